```python
import math
import jax, jax.numpy as jnp
from jax import lax
import numpy as np

D_MODEL = 2048
BATCH = 4
SEQ = 4096
DEPTH = 4

N_MIXERS = 3
MEM_LEN = 256
MIX_WIDTH = D_MODEL * 3 // 4
MEM_HEADS = 4
MEM_HEAD_DIM = (D_MODEL - MIX_WIDTH) // MEM_HEADS
MEM_WIDTH = MEM_HEADS * MEM_HEAD_DIM
RWKV_HEAD_DIM = 64
RWKV_HEADS = MIX_WIDTH // RWKV_HEAD_DIM
DECAY_LORA = 96
AAA_LORA = 96
GATE_LORA = 256
RWKV_GN_EPS = 64e-5
RWKV_SPLITS = (MIX_WIDTH, 2 * MIX_WIDTH, 3 * MIX_WIDTH, 3 * MIX_WIDTH + DECAY_LORA, 3 * MIX_WIDTH + DECAY_LORA + AAA_LORA)
RWKV_COLS = 3 * MIX_WIDTH + DECAY_LORA + AAA_LORA + GATE_LORA
CONV_WIDTH = 3
CONV_COLS = 3 * MIX_WIDTH
SB_HEAD_DIM = 128
SB_HEADS = MIX_WIDTH // SB_HEAD_DIM
SB_COLS = 3 * MIX_WIDTH
Q_BLOCK = 128
N_GROUPS = 4
EXPERTS_PER_GROUP = 8
N_EXPERTS = N_GROUPS * EXPERTS_PER_GROUP
TOP_K = 2
D_EXPERT = 512
DISPATCH_BLOCK = 128
DEEPNORM_ALPHA = (2 * DEPTH) ** 0.25
DEEPNORM_BETA = (8 * DEPTH) ** -0.25
LN_EPS = 1e-5
N_RWKV_LAYERS = (DEPTH + 2) // 3
N_CONV_LAYERS = (DEPTH + 1) // 3
N_SB_LAYERS = DEPTH // 3

kernel_name = "hybrid_rwkv7_shortconv_stickbreaking_hmoe_deepnorm"


def layer_norm(x, g, b):
    xf = x.astype(jnp.float32)
    mu = jnp.mean(xf, axis=-1, keepdims=True)
    var = jnp.mean(jnp.square(xf - mu), axis=-1, keepdims=True)
    return ((xf - mu) * lax.rsqrt(var + LN_EPS) * g + b).astype(x.dtype)


def token_shift(z):
    return jnp.pad(z[:, :-1], ((0, 0), (1, 0), (0, 0)))


def split_heads(t, n_heads):
    B, T, C = t.shape
    return t.reshape(B, T, n_heads, C // n_heads)


def wkv7_scan(r, decay, k, v, kk, a):
    B, T, H, N = r.shape

    def step(S, inp):
        r_t, w_t, k_t, v_t, kk_t, a_t = inp
        sa = jnp.einsum('bhij,bhj->bhi', S, -kk_t)
        S = S * w_t[:, :, None, :] + sa[..., None] * (kk_t * a_t)[:, :, None, :] + v_t[..., None] * k_t[:, :, None, :]
        return S, jnp.einsum('bhij,bhj->bhi', S, r_t)

    xs = tuple(jnp.moveaxis(t, 1, 0) for t in (r, decay, k, v, kk, a))
    _, y = lax.scan(step, jnp.zeros((B, H, N, N), jnp.float32), xs)
    return jnp.moveaxis(y, 0, 1)


def rwkv7_time_mix(z, mu, w0, w2, a0, a2, g2, k_k, k_a, r_k, gn_g, gn_b):
    B, T, _ = z.shape
    z = z + (token_shift(z) - z) * mu
    r, k, v, zw, za, zg = jnp.split(z, RWKV_SPLITS, axis=-1)
    w_log = -jax.nn.softplus(-(w0 + jnp.tanh(zw) @ w2).astype(jnp.float32)) - 0.5
    decay = jnp.exp(-jnp.exp(w_log))
    a = jax.nn.sigmoid((a0 + za @ a2).astype(jnp.float32))
    gate = jax.nn.sigmoid(zg) @ g2
    r, k, v = r.astype(jnp.float32), k.astype(jnp.float32), v.astype(jnp.float32)
    kk = split_heads(k * k_k, RWKV_HEADS)
    kk = kk / jnp.maximum(jnp.linalg.norm(kk, axis=-1, keepdims=True), 1e-12)
    k = k * (1.0 + (a - 1.0) * k_a)
    r, k, v, a, decay = [split_heads(t, RWKV_HEADS) for t in (r, k, v, a, decay)]
    y = wkv7_scan(r, decay, k, v, kk, a)
    mean = jnp.mean(y, axis=-1, keepdims=True)
    var = jnp.mean(jnp.square(y - mean), axis=-1, keepdims=True)
    y = ((y - mean) * lax.rsqrt(var + RWKV_GN_EPS)).reshape(B, T, MIX_WIDTH) * gn_g + gn_b
    bonus = jnp.sum(r * k * r_k, axis=-1, keepdims=True) * v
    y = y + bonus.reshape(B, T, MIX_WIDTH)
    return (y * gate).astype(z.dtype)


def short_conv_mix(z, conv_w):
    b_gate, c_gate, h = jnp.split(z, 3, axis=-1)
    u = c_gate * h
    y = lax.conv_general_dilated(u, conv_w[:, None, :].astype(u.dtype), window_strides=(1,),
                                 padding=[(CONV_WIDTH - 1, 0)], dimension_numbers=('NWC', 'WIO', 'NWC'),
                                 feature_group_count=MIX_WIDTH)
    return b_gate * y


def stick_breaking_mix(z):
    B, T, _ = z.shape
    q, k, v = [jnp.moveaxis(split_heads(t, SB_HEADS), 2, 1) for t in jnp.split(z, 3, axis=-1)]
    n_blocks = T // Q_BLOCK
    qb = jnp.moveaxis(q.reshape(B, SB_HEADS, n_blocks, Q_BLOCK, SB_HEAD_DIM), 2, 0)
    key_pos = jnp.arange(T)
    scale = 1.0 / math.sqrt(SB_HEAD_DIM)

    def block(args):
        i, q_i = args
        logits = jnp.einsum('bhqd,bhkd->bhqk', q_i, k).astype(jnp.float32) * scale
        q_pos = i * Q_BLOCK + jnp.arange(Q_BLOCK)
        causal = key_pos[None, :] < q_pos[:, None]
        log_not = jnp.where(causal, jax.nn.log_sigmoid(-logits), 0.0)
        between = lax.cumsum(log_not, axis=3, reverse=True) - log_not
        attn = jnp.where(causal, jnp.exp(jax.nn.log_sigmoid(logits) + between), 0.0)
        return jnp.einsum('bhqk,bhkd->bhqd', attn.astype(v.dtype), v)

    o = lax.map(block, (jnp.arange(n_blocks), qb))
    return jnp.transpose(o, (1, 0, 3, 2, 4)).reshape(B, T, MIX_WIDTH)


def memory_cross_attention(q, mem, wk, wv):
    B, T, _ = q.shape
    km = split_heads(mem @ wk, MEM_HEADS)
    vm = split_heads(mem @ wv, MEM_HEADS)
    s = jnp.einsum('bthd,bmhd->bhtm', split_heads(q, MEM_HEADS), km).astype(jnp.float32) / math.sqrt(MEM_HEAD_DIM)
    p = jax.nn.softmax(s, axis=-1).astype(vm.dtype)
    return jnp.einsum('bhtm,bmhd->bthd', p, vm).reshape(B, T, MEM_WIDTH)


def hierarchical_moe(x, rg_w, rg_b, re_w, re_b, w_gate, w_up, w_down):
    B, T, D = x.shape
    N = B * T
    xf = x.reshape(N, D)
    g_logits = (xf @ rg_w + rg_b).astype(jnp.float32)
    p_group = jax.nn.softmax(g_logits, axis=-1)
    group = jnp.argmax(g_logits, axis=-1)
    p_group_sel = jnp.take_along_axis(p_group, group[:, None], axis=1)
    e_logits = (xf @ re_w + re_b).astype(jnp.float32).reshape(N, N_GROUPS, EXPERTS_PER_GROUP)
    e_logits = jnp.take_along_axis(e_logits, group[:, None, None], axis=1)[:, 0]
    top_p, top_i = lax.top_k(jax.nn.softmax(e_logits, axis=-1), TOP_K)
    gate = p_group_sel * top_p / jnp.sum(top_p, axis=-1, keepdims=True)
    expert = (group[:, None] * EXPERTS_PER_GROUP + top_i).astype(jnp.int32)
    n_slots = N * TOP_K
    e_flat = expert.reshape(-1)
    g_flat = gate.reshape(-1).astype(x.dtype)
    tok_flat = jnp.arange(n_slots, dtype=jnp.int32) // TOP_K
    counts = jnp.bincount(e_flat, length=N_EXPERTS).astype(jnp.int32)
    padded = (counts + DISPATCH_BLOCK - 1) // DISPATCH_BLOCK * DISPATCH_BLOCK
    pad_end = jnp.cumsum(padded)
    pad_start = pad_end - padded
    start = jnp.cumsum(counts) - counts
    order = jnp.argsort(e_flat)
    e_sorted = e_flat[order]
    dest = pad_start[e_sorted] + jnp.arange(n_slots, dtype=jnp.int32) - start[e_sorted]
    n_blocks = -(-n_slots // DISPATCH_BLOCK) + N_EXPERTS
    cap = n_blocks * DISPATCH_BLOCK
    buf_tok = jnp.full((cap,), N, jnp.int32).at[dest].set(tok_flat[order])
    buf_gate = jnp.zeros((cap,), x.dtype).at[dest].set(g_flat[order])
    block_expert = jnp.minimum(jnp.searchsorted(pad_end, jnp.arange(n_blocks, dtype=jnp.int32) * DISPATCH_BLOCK, side='right'), N_EXPERTS - 1)
    x_pad = jnp.concatenate([xf, jnp.zeros((1, D), x.dtype)], axis=0)
    xb = x_pad[buf_tok].reshape(n_blocks, DISPATCH_BLOCK, D)

    def expert_block(args):
        xe, e = args
        return (jax.nn.silu(xe @ w_gate[e]) * (xe @ w_up[e])) @ w_down[e]

    yb = lax.map(expert_block, (xb, block_expert)).reshape(cap, D) * buf_gate[:, None]
    out = jnp.zeros((N + 1, D), x.dtype).at[buf_tok].add(yb)[:N]
    return out.reshape(B, T, D)


def setup_inputs(seed: int = 0) -> dict:
    key = jax.random.key(seed)
    it = iter(list(jax.random.split(key, 40)))

    def nrm(shape, scale):
        return scale * jax.random.normal(next(it), shape, jnp.float32)

    def unif(shape, lo, hi):
        return jax.random.uniform(next(it), shape, jnp.float32, lo, hi)

    D, C, L = D_MODEL, MIX_WIDTH, DEPTH
    return {
        'x': nrm((BATCH, SEQ, D), 1.0),
        'mem': nrm((BATCH, MEM_LEN, D), 1.0),
        'w_out': nrm((L, C + MEM_WIDTH, D), DEEPNORM_BETA * (C + MEM_WIDTH) ** -0.5),
        'mem_wk': nrm((L, D, MEM_WIDTH), D ** -0.5),
        'mem_wv': nrm((L, D, MEM_WIDTH), DEEPNORM_BETA * D ** -0.5),
        'ln_mix_g': 1.0 + nrm((L, D), 0.02),
        'ln_mix_b': nrm((L, D), 0.02),
        'ln_ffn_g': 1.0 + nrm((L, D), 0.02),
        'ln_ffn_b': nrm((L, D), 0.02),
        'router_group_w': nrm((L, D, N_GROUPS), D ** -0.5),
        'router_group_b': nrm((L, N_GROUPS), 0.01),
        'router_expert_w': nrm((L, D, N_EXPERTS), D ** -0.5),
        'router_expert_b': nrm((L, N_EXPERTS), 0.01),
        'expert_w_gate': nrm((L, N_EXPERTS, D, D_EXPERT), D ** -0.5),
        'expert_w_up': nrm((L, N_EXPERTS, D, D_EXPERT), D ** -0.5),
        'expert_w_down': nrm((L, N_EXPERTS, D_EXPERT, D), DEEPNORM_BETA * D_EXPERT ** -0.5),
        'rwkv_w_in': nrm((N_RWKV_LAYERS, D, RWKV_COLS + MEM_WIDTH), D ** -0.5),
        'rwkv_mu': unif((N_RWKV_LAYERS, RWKV_COLS), 0.0, 1.0),
        'rwkv_w0': unif((N_RWKV_LAYERS, C), -6.0, -1.0),
        'rwkv_w2': nrm((N_RWKV_LAYERS, DECAY_LORA, C), 0.5 * DECAY_LORA ** -0.5),
        'rwkv_a0': nrm((N_RWKV_LAYERS, C), 0.1),
        'rwkv_a2': nrm((N_RWKV_LAYERS, AAA_LORA, C), 0.5 * AAA_LORA ** -0.5),
        'rwkv_g2': nrm((N_RWKV_LAYERS, GATE_LORA, C), GATE_LORA ** -0.5),
        'rwkv_k_k': 0.85 + nrm((N_RWKV_LAYERS, C), 0.05),
        'rwkv_k_a': 1.0 + nrm((N_RWKV_LAYERS, C), 0.05),
        'rwkv_r_k': nrm((N_RWKV_LAYERS, RWKV_HEADS, RWKV_HEAD_DIM), 0.1),
        'rwkv_gn_g': 1.0 + nrm((N_RWKV_LAYERS, C), 0.02),
        'rwkv_gn_b': nrm((N_RWKV_LAYERS, C), 0.02),
        'conv_w_in': nrm((N_CONV_LAYERS, D, CONV_COLS + MEM_WIDTH), D ** -0.5),
        'conv_w': nrm((N_CONV_LAYERS, CONV_WIDTH, C), CONV_WIDTH ** -0.5),
        'sb_w_in': nrm((N_SB_LAYERS, D, SB_COLS + MEM_WIDTH), D ** -0.5),
    }


def reference(x, mem, w_out, mem_wk, mem_wv, ln_mix_g, ln_mix_b, ln_ffn_g, ln_ffn_b,
              router_group_w, router_group_b, router_expert_w, router_expert_b,
              expert_w_gate, expert_w_up, expert_w_down,
              rwkv_w_in, rwkv_mu, rwkv_w0, rwkv_w2, rwkv_a0, rwkv_a2, rwkv_g2,
              rwkv_k_k, rwkv_k_a, rwkv_r_k, rwkv_gn_g, rwkv_gn_b,
              conv_w_in, conv_w, sb_w_in):
    for i in range(DEPTH):
        kind, j = i % N_MIXERS, i // N_MIXERS
        if kind == 0:
            z = x @ rwkv_w_in[j]
            mix = rwkv7_time_mix(z[..., :RWKV_COLS], rwkv_mu[j], rwkv_w0[j], rwkv_w2[j], rwkv_a0[j], rwkv_a2[j],
                                 rwkv_g2[j], rwkv_k_k[j], rwkv_k_a[j], rwkv_r_k[j], rwkv_gn_g[j], rwkv_gn_b[j])
            q_mem = z[..., RWKV_COLS:]
        elif kind == 1:
            z = x @ conv_w_in[j]
            mix = short_conv_mix(z[..., :CONV_COLS], conv_w[j])
            q_mem = z[..., CONV_COLS:]
        else:
            z = x @ sb_w_in[j]
            mix = stick_breaking_mix(z[..., :SB_COLS])
            q_mem = z[..., SB_COLS:]
        mem_out = memory_cross_attention(q_mem, mem, mem_wk[i], mem_wv[i])
        h = jnp.concatenate([mix, mem_out], axis=-1) @ w_out[i]
        x = layer_norm(DEEPNORM_ALPHA * x + h, ln_mix_g[i], ln_mix_b[i])
        f = hierarchical_moe(x, router_group_w[i], router_group_b[i], router_expert_w[i], router_expert_b[i],
                             expert_w_gate[i], expert_w_up[i], expert_w_down[i])
        x = layer_norm(DEEPNORM_ALPHA * x + f, ln_ffn_g[i], ln_ffn_b[i])
    return x
```

```python
import functools
import math

import jax
import jax.numpy as jnp
from jax import lax
from jax.experimental import pallas as pl
from jax.experimental.pallas import tpu as pltpu

F32 = jnp.float32
BF16 = jnp.bfloat16

D_MODEL = 2048
DEPTH = 4
N_MIXERS = 3
MEM_LEN = 256
MIX_WIDTH = 1536
MEM_HEADS = 4
MEM_HEAD_DIM = 128
MEM_WIDTH = 512
RWKV_HEAD_DIM = 64
RWKV_HEADS = 24
DECAY_LORA = 96
AAA_LORA = 96
GATE_LORA = 256
RWKV_GN_EPS = 64e-5
RWKV_COLS = 3 * MIX_WIDTH + DECAY_LORA + AAA_LORA + GATE_LORA
SB_HEAD_DIM = 128
SB_HEADS = 12
N_GROUPS = 4
EXPERTS_PER_GROUP = 8
N_EXPERTS = 32
TOP_K = 2
D_EXPERT = 512
DEEPNORM_ALPHA = (2 * DEPTH) ** 0.25
LN_EPS = 1e-5

LANES = 128
V7X_VMEM_LIMIT_BYTES = 56 << 20

COL_BLOCK = 512
MOE_BLOCK = 256
PLAN_BLOCK = 512
WKV_CHUNK = 64
WKV_PAIR = 2 * RWKV_HEAD_DIM
SB_Q_BLOCK = 256
SB_K_BLOCK = 128


def _params(*sem):
    return pltpu.CompilerParams(dimension_semantics=sem, vmem_limit_bytes=V7X_VMEM_LIMIT_BYTES)


def _dot(a, b):
    return jnp.dot(a, b, preferred_element_type=F32)


def _dot_nt(a, b):
    return lax.dot_general(a, b, (((1,), (1,)), ((), ())), preferred_element_type=F32)


def _dot_tn(a, b):
    return lax.dot_general(a, b, (((0,), (0,)), ((), ())), preferred_element_type=F32)


def _split2(x):
    hi = x.astype(BF16)
    lo = (x - hi.astype(F32)).astype(BF16)
    return hi, lo


def _dot_hl(x, w):
    hi, lo = _split2(x)
    return _dot(hi, w) + _dot(lo, w)


def _pick_tile(n, target, quantum=LANES):
    best = None
    for t in range(quantum, min(n, target) + 1, quantum):
        if n % t == 0:
            best = t
    assert best is not None, (n, target)
    return best


def _layer_norm_rows(y, g, b):
    mu = jnp.mean(y, axis=-1, keepdims=True)
    yc = y - mu
    var = jnp.mean(yc * yc, axis=-1, keepdims=True)
    return yc * lax.rsqrt(var + LN_EPS) * g + b


def _matmul_kernel(x_ref, w_ref, o_ref):
    o_ref[...] = _dot(x_ref[...], w_ref[...]).astype(o_ref.dtype)


def _matmul(x, w, out_dtype, name):
    m, k = x.shape
    n = w.shape[1]
    tm = _pick_tile(m, 1024, 8)
    tn = _pick_tile(n, 1536)
    return pl.pallas_call(
        _matmul_kernel,
        grid=(n // tn, m // tm),
        in_specs=[pl.BlockSpec((tm, k), lambda j, i: (i, 0)),
                  pl.BlockSpec((k, tn), lambda j, i: (0, j))],
        out_specs=pl.BlockSpec((tm, tn), lambda j, i: (i, j)),
        out_shape=jax.ShapeDtypeStruct((m, n), out_dtype),
        compiler_params=_params("parallel", "arbitrary"),
        name=name,
    )(x, w)


def _mem_attn_kernel(q_ref, kv_ref, o_ref):
    scale = 1.0 / math.sqrt(MEM_HEAD_DIM)
    for h in range(MEM_HEADS):
        sl = slice(h * MEM_HEAD_DIM, (h + 1) * MEM_HEAD_DIM)
        q = q_ref[:, sl]
        km = kv_ref[:, sl]
        vm = kv_ref[:, MEM_WIDTH + h * MEM_HEAD_DIM:MEM_WIDTH + (h + 1) * MEM_HEAD_DIM]
        s = _dot_nt(q, km) * scale
        s = s - jnp.max(s, axis=-1, keepdims=True)
        p = jnp.exp(s)
        p = p / jnp.sum(p, axis=-1, keepdims=True)
        o_ref[:, sl] = _dot(p.astype(BF16), vm).astype(o_ref.dtype)


def _mem_attention(z, kv, batch, seq, q_col_block):
    tq = _pick_tile(seq, 512, 8)
    nt = seq // tq
    return pl.pallas_call(
        _mem_attn_kernel,
        grid=(batch, nt),
        in_specs=[pl.BlockSpec((tq, MEM_WIDTH), lambda b, i: (b * nt + i, q_col_block)),
                  pl.BlockSpec((MEM_LEN, 2 * MEM_WIDTH), lambda b, i: (b, 0))],
        out_specs=pl.BlockSpec((tq, MEM_WIDTH), lambda b, i: (b * nt + i, 0)),
        out_shape=jax.ShapeDtypeStruct((batch * seq, MEM_WIDTH), BF16),
        compiler_params=_params("parallel", "parallel"),
        name="mem_attention",
    )(z, kv)


def _outproj_ln_kernel(mix_ref, mem_ref, w1_ref, w2_ref, x_ref, g_ref, b_ref, xo_ref, xb_ref):
    h = _dot(mix_ref[...], w1_ref[...]) + _dot(mem_ref[...], w2_ref[...])
    y = DEEPNORM_ALPHA * x_ref[...] + h
    out = _layer_norm_rows(y, g_ref[...], b_ref[...])
    xo_ref[...] = out
    xb_ref[...] = out.astype(BF16)


def _outproj_ln(mix, mem_out, w_mix, w_mem, x, g, b):
    n = x.shape[0]
    tm = _pick_tile(n, 512, 8)
    row = lambda i: (i, 0)
    fixed = lambda i: (0, 0)
    return pl.pallas_call(
        _outproj_ln_kernel,
        grid=(n // tm,),
        in_specs=[pl.BlockSpec((tm, MIX_WIDTH), row),
                  pl.BlockSpec((tm, MEM_WIDTH), row),
                  pl.BlockSpec((MIX_WIDTH, D_MODEL), fixed),
                  pl.BlockSpec((MEM_WIDTH, D_MODEL), fixed),
                  pl.BlockSpec((tm, D_MODEL), row),
                  pl.BlockSpec((1, D_MODEL), fixed),
                  pl.BlockSpec((1, D_MODEL), fixed)],
        out_specs=[pl.BlockSpec((tm, D_MODEL), row), pl.BlockSpec((tm, D_MODEL), row)],
        out_shape=[jax.ShapeDtypeStruct((n, D_MODEL), F32), jax.ShapeDtypeStruct((n, D_MODEL), BF16)],
        compiler_params=_params("parallel"),
        name="outproj_ln",
    )(mix, mem_out, w_mix, w_mem, x, g, b)


def _conv_kernel(b_ref, c_ref, h_ref, w_ref, o_ref, carry_ref):
    @pl.when(pl.program_id(1) == 0)
    def _():
        carry_ref[...] = jnp.zeros_like(carry_ref)

    u = c_ref[...].astype(F32) * h_ref[...].astype(F32)
    tt = u.shape[0]
    rowid = lax.broadcasted_iota(jnp.int32, u.shape, 0)
    c0 = carry_ref[0:1, :]
    c1 = carry_ref[1:2, :]
    u1 = jnp.where(rowid == 0, c1, pltpu.roll(u, 1, axis=0))
    u2 = jnp.where(rowid == 0, c0, jnp.where(rowid == 1, c1, pltpu.roll(u, 2, axis=0)))
    carry_ref[0:2, :] = u[tt - 2:tt, :]
    y = w_ref[0:1, :] * u2 + w_ref[1:2, :] * u1 + w_ref[2:3, :] * u
    o_ref[...] = (b_ref[...].astype(F32) * y).astype(o_ref.dtype)


def _short_conv(z, conv_w, batch, seq):
    tt = _pick_tile(seq, 512, 8)
    nt = seq // tt
    spec = lambda c: pl.BlockSpec((tt, MIX_WIDTH), lambda b, i: (b * nt + i, c))
    return pl.pallas_call(
        _conv_kernel,
        grid=(batch, nt),
        in_specs=[spec(0), spec(1), spec(2),
                  pl.BlockSpec((3, MIX_WIDTH), lambda b, i: (0, 0))],
        out_specs=pl.BlockSpec((tt, MIX_WIDTH), lambda b, i: (b * nt + i, 0)),
        out_shape=jax.ShapeDtypeStruct((batch * seq, MIX_WIDTH), BF16),
        scratch_shapes=[pltpu.VMEM((8, MIX_WIDTH), F32)],
        compiler_params=_params("parallel", "arbitrary"),
        name="short_conv",
    )(z, z, z, conv_w)


def _sb_kernel(q_ref, k_ref, v_ref, o_ref):
    i = pl.program_id(2)
    tq = q_ref.shape[0]
    bk = SB_K_BLOCK
    scale = 1.0 / math.sqrt(SB_HEAD_DIM)
    q = q_ref[...]
    n_kv = (i + 1) * (tq // bk)
    q_pos = i * tq + lax.broadcasted_iota(jnp.int32, (tq, bk), 0)
    k_off = lax.broadcasted_iota(jnp.int32, (tq, bk), 1)
    r = lax.broadcasted_iota(jnp.int32, (2 * bk, bk), 0)
    c = lax.broadcasted_iota(jnp.int32, (2 * bk, bk), 1)
    suffix = jnp.where((r & (bk - 1)) > c, 1.0, 0.0).astype(BF16)

    def body(step, carry):
        acc, later = carry
        j = n_kv - 1 - step
        start = pl.multiple_of(j * bk, bk)
        ks = k_ref[pl.ds(start, bk), :]
        vs = v_ref[pl.ds(start, bk), :]
        s = _dot_nt(q, ks) * scale
        valid = (j * bk + k_off) < q_pos
        soft = jnp.log(1.0 + jnp.exp(-jnp.abs(s)))
        log_beta = jnp.minimum(s, 0.0) - soft
        log_not = jnp.where(valid, jnp.minimum(-s, 0.0) - soft, 0.0)
        hi, lo = _split2(log_not)
        between = _dot(jnp.concatenate([hi, lo], axis=1), suffix) + later
        attn = jnp.where(valid, jnp.exp(log_beta + between), 0.0)
        acc = acc + _dot(attn.astype(BF16), vs)
        later = later + jnp.sum(log_not, axis=-1, keepdims=True)
        return acc, later

    acc, _ = lax.fori_loop(0, n_kv, body, (jnp.zeros((tq, SB_HEAD_DIM), F32), jnp.zeros((tq, 1), F32)))
    o_ref[...] = acc.astype(o_ref.dtype)


def _stick_breaking(z, batch, seq):
    tq = _pick_tile(seq, SB_Q_BLOCK, SB_K_BLOCK)
    nq = seq // tq
    return pl.pallas_call(
        _sb_kernel,
        grid=(batch, SB_HEADS, nq),
        in_specs=[pl.BlockSpec((tq, SB_HEAD_DIM), lambda b, h, i: (b * nq + i, h)),
                  pl.BlockSpec((seq, SB_HEAD_DIM), lambda b, h, i: (b, SB_HEADS + h)),
                  pl.BlockSpec((seq, SB_HEAD_DIM), lambda b, h, i: (b, 2 * SB_HEADS + h))],
        out_specs=pl.BlockSpec((tq, SB_HEAD_DIM), lambda b, h, i: (b * nq + i, h)),
        out_shape=jax.ShapeDtypeStruct((batch * seq, MIX_WIDTH), BF16),
        compiler_params=_params("parallel", "parallel", "arbitrary"),
        name="stick_breaking",
    )(z, z, z)


def _rwkv_prep_kernel(zr_ref, zk_ref, zv_ref, zl_ref, mur_ref, muk_ref, muv_ref, mul_ref,
                      w2_ref, a2_ref, g2_ref, w0_ref, a0_ref, kk_ref, ka_ref, rk_ref,
                      seg_ref, segt_ref,
                      r_out, k_out, v_out, an_out, bb_out, lw_out, gate_out, bonus_out,
                      carry_ref):
    @pl.when(pl.program_id(2) == 0)
    def _():
        carry_ref[...] = jnp.zeros_like(carry_ref)

    def shifted(z_ref, slot, mu_ref):
        z = z_ref[...].astype(F32)
        tt = z.shape[0]
        rowid = lax.broadcasted_iota(jnp.int32, z.shape, 0)
        prev = jnp.where(rowid == 0, carry_ref[slot:slot + 1, :], pltpu.roll(z, 1, axis=0))
        carry_ref[slot:slot + 1, :] = z[tt - 1:tt, :]
        return z + (prev - z) * mu_ref[...]

    r = shifted(zr_ref, 0, mur_ref)
    k = shifted(zk_ref, 1, muk_ref)
    v = shifted(zv_ref, 2, muv_ref)
    zl = shifted(zl_ref, 3, mul_ref)
    zw = zl[:, 0:LANES]
    za = zl[:, LANES:2 * LANES]
    zg = zl[:, 2 * LANES:4 * LANES]

    dw = w0_ref[...] + _dot(jnp.tanh(zw).astype(BF16), w2_ref[...])
    neg = -dw
    softplus = jnp.maximum(neg, 0.0) + jnp.log(1.0 + jnp.exp(-jnp.abs(neg)))
    lw = -jnp.exp(-softplus - 0.5)
    a = jax.nn.sigmoid(a0_ref[...] + _dot(za.astype(BF16), a2_ref[...]))
    gate = _dot(jax.nn.sigmoid(zg).astype(BF16), g2_ref[...])

    seg = seg_ref[...]
    segt = segt_ref[...]
    kk = k * kk_ref[...]
    norm = jnp.sqrt(_dot_hl(_dot_hl(kk * kk, seg), segt))
    kkn = kk / jnp.maximum(norm, 1e-12)
    k2 = k * (1.0 + (a - 1.0) * ka_ref[...])
    bonus = _dot_hl(_dot_hl(r * k2 * rk_ref[...], seg), segt) * v

    r_out[...] = r
    k_out[...] = k2
    v_out[...] = v
    an_out[...] = -kkn
    bb_out[...] = kkn * a
    lw_out[...] = lw
    gate_out[...] = gate
    bonus_out[...] = bonus


def _rwkv_prep(z, mu_main, mu_lora, w2p, a2p, g2, w0, a0, k_k, k_a, r_k, batch, seq):
    n = batch * seq
    tt = _pick_tile(seq, 256, 8)
    nt = seq // tt
    nc = MIX_WIDTH // COL_BLOCK
    heads_per_block = COL_BLOCK // RWKV_HEAD_DIM
    ch = jnp.arange(COL_BLOCK)[:, None] // RWKV_HEAD_DIM
    seg = (ch == jnp.arange(LANES)[None, :]).astype(BF16)
    segt = seg.T
    assert heads_per_block <= LANES

    zspec = lambda off: pl.BlockSpec((tt, COL_BLOCK), lambda c, b, i: (b * nt + i, off + c))
    zlspec = pl.BlockSpec((tt, COL_BLOCK), lambda c, b, i: (b * nt + i, 3 * nc + 1))
    muspec = lambda off: pl.BlockSpec((1, COL_BLOCK), lambda c, b, i: (0, off + c))
    colspec = lambda rows: pl.BlockSpec((rows, COL_BLOCK), lambda c, b, i: (0, c))
    fixed = lambda shape: pl.BlockSpec(shape, lambda c, b, i: (0, 0))
    ospec = pl.BlockSpec((tt, COL_BLOCK), lambda c, b, i: (b * nt + i, c))
    outs = pl.pallas_call(
        _rwkv_prep_kernel,
        grid=(nc, batch, nt),
        in_specs=[zspec(0), zspec(nc), zspec(2 * nc), zlspec,
                  muspec(0), muspec(nc), muspec(2 * nc), fixed((1, COL_BLOCK)),
                  colspec(LANES), colspec(LANES), colspec(GATE_LORA),
                  colspec(1), colspec(1), colspec(1), colspec(1), colspec(1),
                  fixed((COL_BLOCK, LANES)), fixed((LANES, COL_BLOCK))],
        out_specs=[ospec] * 8,
        out_shape=[jax.ShapeDtypeStruct((n, MIX_WIDTH), F32)] * 8,
        scratch_shapes=[pltpu.VMEM((8, COL_BLOCK), F32)],
        compiler_params=_params("parallel", "parallel", "arbitrary"),
        name="rwkv_prep",
    )(z, z, z, z, mu_main, mu_main, mu_main, mu_lora, w2p, a2p, g2, w0, a0, k_k, k_a, r_k, seg, segt)
    return outs


def _wkv_chunk_terms(r, k, v, an, bb, lw, consts):
    tri, lane_lo, strict, incl, blk16, eye = consts
    c = WKV_CHUNK
    h1 = lw.astype(BF16)
    r1 = lw - h1.astype(F32)
    h2 = r1.astype(BF16)
    h3 = (r1 - h2.astype(F32)).astype(BF16)
    cum = _dot(tri, h1) + _dot(tri, h2) + _dot(tri, h3)
    cum_last = cum[c - 1:c, :]
    p_in = jnp.exp(cum)
    p_ex = jnp.exp(cum - lw)
    p_inv = jnp.exp(-cum)
    p_tail = jnp.exp(cum_last - cum)
    pc = jnp.exp(cum_last)

    def pair_rows(x):
        return jnp.concatenate([jnp.where(lane_lo, x, 0.0), jnp.where(lane_lo, 0.0, x)], axis=0)

    a_p = pair_rows(an * p_ex).astype(BF16)
    q_p = pair_rows(r * p_in)
    b_p = pair_rows(bb * p_inv).astype(BF16)
    k_p = pair_rows(k * p_inv).astype(BF16)
    v_p = pair_rows(v).astype(BF16)
    bt_p = pair_rows(bb * p_tail).astype(BF16)
    kt_p = pair_rows(k * p_tail).astype(BF16)
    q_pb = q_p.astype(BF16)

    aq = jnp.concatenate([a_p, q_pb], axis=0)
    bk = jnp.concatenate([b_p, k_p], axis=0)
    g = _dot_nt(aq, bk)
    n2 = 2 * c
    l_ab = jnp.where(strict, g[:n2, :n2], 0.0)
    a_ak = jnp.where(strict, g[:n2, n2:], 0.0)
    a_rb = jnp.where(incl, g[n2:, :n2], 0.0)
    a_rk = jnp.where(incl, g[n2:, n2:], 0.0)

    def mm(x, y):
        return _dot(x.astype(BF16), y.astype(BF16))

    ld = jnp.where(blk16, l_ab, 0.0)
    lo = l_ab - ld
    l2 = mm(ld, ld)
    l4 = mm(l2, l2)
    l8 = mm(l4, l4)
    dinv = mm(eye + ld, eye + l2)
    dinv = mm(dinv, eye + l4)
    dinv = mm(dinv, eye + l8)
    m1 = mm(dinv, lo)
    m2 = mm(m1, m1)
    tinv = mm(mm(eye + m1, eye + m2), dinv)

    w_p = mm(tinv, a_p)
    av = _dot(jnp.concatenate([a_ak, a_rk], axis=0).astype(BF16), v_p)
    u0 = mm(tinv, av[:n2])
    qw = q_p + mm(a_rb, w_p)
    y0 = mm(a_rb, u0) + av[n2:]
    e = _dot_tn(w_p.astype(BF16), bt_p)
    f = _dot_tn(jnp.concatenate([u0.astype(BF16), v_p], axis=0), jnp.concatenate([bt_p, kt_p], axis=0))
    return qw[:c] + qw[c:], y0[:c] + y0[c:], e, f, pc


def _wkv_terms_kernel(r_ref, k_ref, v_ref, an_ref, bb_ref, lw_ref, qw_ref, y0_ref, e_ref, f_ref, pc_ref):
    c = WKV_CHUNK
    n2 = 2 * c
    row = lax.broadcasted_iota(jnp.int32, (n2, n2), 0)
    col = lax.broadcasted_iota(jnp.int32, (n2, n2), 1)
    tri_r = lax.broadcasted_iota(jnp.int32, (c, c), 0)
    tri_c = lax.broadcasted_iota(jnp.int32, (c, c), 1)
    consts = (
        jnp.where(tri_c <= tri_r, 1.0, 0.0).astype(BF16),
        lax.broadcasted_iota(jnp.int32, (c, WKV_PAIR), 1) < RWKV_HEAD_DIM,
        row > col,
        row >= col,
        (row >> 4) == (col >> 4),
        jnp.where(row == col, 1.0, 0.0),
    )
    n_chunks = r_ref.shape[0] // c
    for j in range(n_chunks):
        rows = slice(j * c, (j + 1) * c)
        qw, y0, e, f, pc = _wkv_chunk_terms(r_ref[rows, :], k_ref[rows, :], v_ref[rows, :], an_ref[rows, :],
                                            bb_ref[rows, :], lw_ref[rows, :], consts)
        qw_ref[rows, :] = qw.astype(qw_ref.dtype)
        y0_ref[rows, :] = y0
        e_ref[j * n2:(j + 1) * n2, :] = e.astype(e_ref.dtype)
        f_ref[j * n2:(j + 1) * n2, :] = f
        pc_ref[j * 8:(j + 1) * 8, :] = jnp.broadcast_to(pc, (8, WKV_PAIR))


def _wkv_terms(r, k, v, an, bb, lw, batch, seq):
    n = batch * seq
    c = WKV_CHUNK
    n_pairs = MIX_WIDTH // WKV_PAIR
    cps = 4 if (seq // c) % 4 == 0 else 1
    steps = seq // (c * cps)
    tok = pl.BlockSpec((cps * c, WKV_PAIR), lambda b, p, i: (b * steps + i, p))
    mat = pl.BlockSpec((cps * WKV_PAIR, WKV_PAIR), lambda b, p, i: (b * steps + i, p))
    vec = pl.BlockSpec((cps * 8, WKV_PAIR), lambda b, p, i: (b * steps + i, p))
    n_chunks = n // c
    return pl.pallas_call(
        _wkv_terms_kernel,
        grid=(batch, n_pairs, steps),
        in_specs=[tok] * 6,
        out_specs=[tok, tok, mat, mat, vec],
        out_shape=[jax.ShapeDtypeStruct((n, MIX_WIDTH), BF16),
                   jax.ShapeDtypeStruct((n, MIX_WIDTH), F32),
                   jax.ShapeDtypeStruct((n_chunks * WKV_PAIR, MIX_WIDTH), BF16),
                   jax.ShapeDtypeStruct((n_chunks * WKV_PAIR, MIX_WIDTH), F32),
                   jax.ShapeDtypeStruct((n_chunks * 8, MIX_WIDTH), F32)],
        compiler_params=_params("parallel", "parallel", "parallel"),
        name="wkv_terms",
    )(r, k, v, an, bb, lw)


def _wkv_scan_kernel(qw_ref, y0_ref, e_ref, f_ref, pc_ref, bonus_ref, gate_ref, gng_ref, gnb_ref,
                     o_ref, state_ref):
    @pl.when(pl.program_id(2) == 0)
    def _():
        state_ref[...] = jnp.zeros_like(state_ref)

    n_pairs = state_ref.shape[1] // WKV_PAIR
    lane_lo = lax.broadcasted_iota(jnp.int32, (WKV_CHUNK, WKV_PAIR), 1) < RWKV_HEAD_DIM
    inv_n = 1.0 / RWKV_HEAD_DIM
    for p in range(n_pairs):
        sl = slice(p * WKV_PAIR, (p + 1) * WKV_PAIR)
        s = state_ref[:, sl]
        sb = s.astype(BF16)
        y = _dot_nt(qw_ref[:, sl], sb) + y0_ref[:, sl]
        state_ref[:, sl] = s * pc_ref[0:1, sl] + _dot(sb, e_ref[:, sl]) + f_ref[:, sl]

        def head_mean(x):
            lo = jnp.sum(jnp.where(lane_lo, x, 0.0), axis=-1, keepdims=True)
            hi = jnp.sum(jnp.where(lane_lo, 0.0, x), axis=-1, keepdims=True)
            return jnp.where(lane_lo, lo, hi) * inv_n

        yc = y - head_mean(y)
        var = head_mean(yc * yc)
        yn = yc * lax.rsqrt(var + RWKV_GN_EPS) * gng_ref[:, sl] + gnb_ref[:, sl]
        o_ref[:, sl] = ((yn + bonus_ref[:, sl]) * gate_ref[:, sl]).astype(o_ref.dtype)


def _wkv_scan(qw, y0, e, f, pc, bonus, gate, gn_g, gn_b, batch, seq):
    n = batch * seq
    c = WKV_CHUNK
    nc = seq // c
    width = MIX_WIDTH // 2
    groups = MIX_WIDTH // width
    tok = pl.BlockSpec((c, width), lambda b, g, i: (b * nc + i, g))
    mat = pl.BlockSpec((WKV_PAIR, width), lambda b, g, i: (b * nc + i, g))
    vec = pl.BlockSpec((8, width), lambda b, g, i: (b * nc + i, g))
    par = pl.BlockSpec((1, width), lambda b, g, i: (0, g))
    return pl.pallas_call(
        _wkv_scan_kernel,
        grid=(batch, groups, nc),
        in_specs=[tok, tok, mat, mat, vec, tok, tok, par, par],
        out_specs=tok,
        out_shape=jax.ShapeDtypeStruct((n, MIX_WIDTH), BF16),
        scratch_shapes=[pltpu.VMEM((WKV_PAIR, width), F32)],
        compiler_params=_params("parallel", "parallel", "arbitrary"),
        name="wkv_scan",
    )(qw, y0, e, f, pc, bonus, gate, gn_g, gn_b)


def _router_kernel(x_ref, w_ref, b_ref, o_ref):
    logits = jnp.dot(x_ref[...], w_ref[...], preferred_element_type=F32,
                     precision=lax.Precision.HIGHEST) + b_ref[...]
    lane = lax.broadcasted_iota(jnp.int32, logits.shape, 1)
    neg_inf = -jnp.inf
    big = 4 * LANES

    def first_where(mask):
        return jnp.min(jnp.where(mask, lane, big), axis=-1, keepdims=True)

    gl = jnp.where(lane < N_GROUPS, logits, neg_inf)
    gmax = jnp.max(gl, axis=-1, keepdims=True)
    gidx = first_where(gl == gmax)
    p_group_sel = 1.0 / jnp.sum(jnp.exp(gl - gmax), axis=-1, keepdims=True)

    lo = N_GROUPS + EXPERTS_PER_GROUP * gidx
    emask = (lane >= lo) & (lane < lo + EXPERTS_PER_GROUP)
    el = jnp.where(emask, logits, neg_inf)
    emax = jnp.max(el, axis=-1, keepdims=True)
    pe = jnp.exp(el - emax)
    pe = pe / jnp.sum(pe, axis=-1, keepdims=True)
    pe = jnp.where(emask, pe, -1.0)
    p1 = jnp.max(pe, axis=-1, keepdims=True)
    i1 = first_where(pe == p1)
    pe2 = jnp.where(lane == i1, -1.0, pe)
    p2 = jnp.max(pe2, axis=-1, keepdims=True)
    i2 = first_where(pe2 == p2)
    denom = p1 + p2
    g1 = p_group_sel * p1 / denom
    g2 = p_group_sel * p2 / denom
    e1 = (i1 - N_GROUPS).astype(F32)
    e2 = (i2 - N_GROUPS).astype(F32)
    o_ref[...] = jnp.where(lane == 0, e1, jnp.where(lane == 1, e2, jnp.where(lane == 2, g1, jnp.where(lane == 3, g2, 0.0))))


def _router(x, w_router, b_router):
    n = x.shape[0]
    tm = _pick_tile(n, 512, 8)
    return pl.pallas_call(
        _router_kernel,
        grid=(n // tm,),
        in_specs=[pl.BlockSpec((tm, D_MODEL), lambda i: (i, 0)),
                  pl.BlockSpec((D_MODEL, LANES), lambda i: (0, 0)),
                  pl.BlockSpec((1, LANES), lambda i: (0, 0))],
        out_specs=pl.BlockSpec((tm, LANES), lambda i: (i, 0)),
        out_shape=jax.ShapeDtypeStruct((n, LANES), F32),
        compiler_params=_params("parallel"),
        name="moe_router",
    )(x, w_router, b_router)


def _plan_kernel(e_ref, rank_ref, count_ref, carry_ref):
    @pl.when(pl.program_id(0) == 0)
    def _():
        carry_ref[...] = jnp.zeros_like(carry_ref)

    nb = e_ref.shape[-1]
    e = e_ref[0]
    onehot = jnp.where(lax.broadcasted_iota(jnp.int32, (N_EXPERTS, nb), 0) == e, 1.0, 0.0)
    r = lax.broadcasted_iota(jnp.int32, (nb, nb), 0)
    c = lax.broadcasted_iota(jnp.int32, (nb, nb), 1)
    before = jnp.where(r < c, 1.0, 0.0).astype(BF16)
    seen = _dot(onehot.astype(BF16), before) + carry_ref[:, 0:1]
    rank_ref[0] = jnp.sum(onehot * seen, axis=0, keepdims=True).astype(jnp.int32)
    carry_ref[...] = carry_ref[...] + jnp.sum(onehot, axis=1, keepdims=True)
    count_ref[...] = carry_ref[...]


def _dispatch_plan(e_flat):
    n_slots = e_flat.shape[0]
    nb = _pick_tile(n_slots, PLAN_BLOCK)
    steps = n_slots // nb
    rank, counts = pl.pallas_call(
        _plan_kernel,
        grid=(steps,),
        in_specs=[pl.BlockSpec((1, 1, nb), lambda i: (i, 0, 0))],
        out_specs=[pl.BlockSpec((1, 1, nb), lambda i: (i, 0, 0)),
                   pl.BlockSpec((N_EXPERTS, LANES), lambda i: (0, 0))],
        out_shape=[jax.ShapeDtypeStruct((steps, 1, nb), jnp.int32),
                   jax.ShapeDtypeStruct((N_EXPERTS, LANES), F32)],
        scratch_shapes=[pltpu.VMEM((N_EXPERTS, LANES), F32)],
        compiler_params=_params("arbitrary"),
        name="moe_plan",
    )(e_flat.reshape(steps, 1, nb))
    return rank.reshape(n_slots), counts[:, 0].astype(jnp.int32)


def _row_copy(table_hbm, dst_ref, src_row, dst_row, sem):
    return pltpu.make_async_copy(table_hbm.at[pl.ds(src_row, 1)], dst_ref.at[pl.ds(dst_row, 1)], sem)


def _gather_rows_into(idx_ref, offset, stride, table_hbm, dst_ref, sem):
    rows = dst_ref.shape[0]

    def start(i, carry):
        _row_copy(table_hbm, dst_ref, idx_ref[0, 0, offset + stride * i], i, sem).start()
        return carry

    lax.fori_loop(0, rows, start, 0)

    def wait(i, carry):
        _row_copy(table_hbm, dst_ref, 0, i, sem).wait()
        return carry

    lax.fori_loop(0, rows, wait, 0)


def _gather_kernel(idx_ref, table_hbm, o_ref, sem):
    _gather_rows_into(idx_ref, 0, 1, table_hbm, o_ref, sem.at[0])


def _gather_rows(table, idx, rows_per_step):
    n_out = idx.shape[0]
    steps = n_out // rows_per_step
    width = table.shape[1]
    return pl.pallas_call(
        _gather_kernel,
        grid=(steps,),
        in_specs=[pl.BlockSpec((1, 1, rows_per_step), lambda i: (i, 0, 0), memory_space=pltpu.SMEM),
                  pl.BlockSpec(memory_space=pl.ANY)],
        out_specs=pl.BlockSpec((rows_per_step, width), lambda i: (i, 0)),
        out_shape=jax.ShapeDtypeStruct((n_out, width), table.dtype),
        scratch_shapes=[pltpu.SemaphoreType.DMA((1,))],
        compiler_params=_params("arbitrary"),
        name="moe_dispatch_gather",
    )(idx.reshape(steps, 1, rows_per_step), table)


def _expert_kernel(be_ref, nu_ref, x_ref, wg_ref, wu_ref, wd_ref, o_ref, wg_s, wu_s, wd_s):
    i = pl.program_id(0)
    prev = be_ref[jnp.maximum(i - 1, 0)]
    changed = (i == 0) | (be_ref[i] != prev)

    @pl.when(changed)
    def _():
        wg_s[...] = wg_ref[0].astype(BF16)
        wu_s[...] = wu_ref[0].astype(BF16)
        wd_s[...] = wd_ref[0].astype(BF16)

    used = i < nu_ref[0]

    @pl.when(used)
    def _():
        xb = x_ref[...].astype(BF16)
        g = _dot(xb, wg_s[...])
        u = _dot(xb, wu_s[...])
        h = (g * jax.nn.sigmoid(g)) * u
        o_ref[...] = _dot(h.astype(BF16), wd_s[...])

    @pl.when(jnp.logical_not(used))
    def _():
        o_ref[...] = jnp.zeros_like(o_ref)


def _experts(xs, block_expert, n_used, w_gate, w_up, w_down):
    cap = xs.shape[0]
    n_blk = cap // MOE_BLOCK
    wspec = lambda shape: pl.BlockSpec((1,) + shape, lambda i, be, nu: (be[i], 0, 0))
    grid_spec = pltpu.PrefetchScalarGridSpec(
        num_scalar_prefetch=2,
        grid=(n_blk,),
        in_specs=[pl.BlockSpec((MOE_BLOCK, D_MODEL), lambda i, be, nu: (i, 0)),
                  wspec((D_MODEL, D_EXPERT)), wspec((D_MODEL, D_EXPERT)), wspec((D_EXPERT, D_MODEL))],
        out_specs=pl.BlockSpec((MOE_BLOCK, D_MODEL), lambda i, be, nu: (i, 0)),
        scratch_shapes=[pltpu.VMEM((D_MODEL, D_EXPERT), BF16),
                        pltpu.VMEM((D_MODEL, D_EXPERT), BF16),
                        pltpu.VMEM((D_EXPERT, D_MODEL), BF16)],
    )
    return pl.pallas_call(
        _expert_kernel,
        grid_spec=grid_spec,
        out_shape=jax.ShapeDtypeStruct((cap, D_MODEL), F32),
        compiler_params=_params("arbitrary"),
        name="moe_experts",
    )(block_expert, n_used, xs, w_gate, w_up, w_down)


def _combine_ln_kernel(idx_ref, ys_hbm, slab_ref, x_ref, g_ref, b_ref, xo_ref, xb_ref, buf_a, buf_b, sem):
    _gather_rows_into(idx_ref, 0, 2, ys_hbm, buf_a, sem.at[0])
    _gather_rows_into(idx_ref, 1, 2, ys_hbm, buf_b, sem.at[1])
    slab = slab_ref[...]
    lane = lax.broadcasted_iota(jnp.int32, slab.shape, 1)
    g1 = jnp.sum(jnp.where(lane == 2, slab, 0.0), axis=-1, keepdims=True)
    g2 = jnp.sum(jnp.where(lane == 3, slab, 0.0), axis=-1, keepdims=True)
    y = DEEPNORM_ALPHA * x_ref[...] + (buf_a[...] * g1 + buf_b[...] * g2)
    out = _layer_norm_rows(y, g_ref[...], b_ref[...])
    xo_ref[...] = out
    xb_ref[...] = out.astype(BF16)


def _combine_ln(ys, dest, slab, x, g, b):
    n = x.shape[0]
    tm = _pick_tile(n, 256, 8)
    steps = n // tm
    row = lambda i: (i, 0)
    fixed = lambda i: (0, 0)
    return pl.pallas_call(
        _combine_ln_kernel,
        grid=(steps,),
        in_specs=[pl.BlockSpec((1, 1, TOP_K * tm), lambda i: (i, 0, 0), memory_space=pltpu.SMEM),
                  pl.BlockSpec(memory_space=pl.ANY),
                  pl.BlockSpec((tm, LANES), row),
                  pl.BlockSpec((tm, D_MODEL), row),
                  pl.BlockSpec((1, D_MODEL), fixed),
                  pl.BlockSpec((1, D_MODEL), fixed)],
        out_specs=[pl.BlockSpec((tm, D_MODEL), row), pl.BlockSpec((tm, D_MODEL), row)],
        out_shape=[jax.ShapeDtypeStruct((n, D_MODEL), F32), jax.ShapeDtypeStruct((n, D_MODEL), BF16)],
        scratch_shapes=[pltpu.VMEM((tm, D_MODEL), F32), pltpu.VMEM((tm, D_MODEL), F32),
                        pltpu.SemaphoreType.DMA((2,))],
        compiler_params=_params("arbitrary"),
        name="moe_combine_ln",
    )(dest.reshape(steps, 1, TOP_K * tm), ys, slab, x, g, b)


def _moe_ln(x, rg_w, rg_b, re_w, re_b, w_gate, w_up, w_down, ln_g, ln_b):
    n = x.shape[0]
    n_slots = n * TOP_K
    pad_cols = LANES - N_GROUPS - N_EXPERTS
    w_router = jnp.concatenate([rg_w, re_w, jnp.zeros((D_MODEL, pad_cols), F32)], axis=1)
    b_router = jnp.concatenate([rg_b, re_b, jnp.zeros((pad_cols,), F32)])[None, :]
    slab = _router(x, w_router, b_router)

    e_flat = slab[:, :TOP_K].astype(jnp.int32).reshape(n_slots)
    rank, counts = _dispatch_plan(e_flat)
    padded = (counts + MOE_BLOCK - 1) // MOE_BLOCK * MOE_BLOCK
    pad_end = jnp.cumsum(padded)
    pad_start = pad_end - padded
    dest = pad_start[e_flat] + rank
    cap = n_slots + N_EXPERTS * MOE_BLOCK
    n_blk = cap // MOE_BLOCK
    src_tok = jnp.zeros((cap,), jnp.int32).at[dest].set(jnp.arange(n_slots, dtype=jnp.int32) // TOP_K)
    block_expert = jnp.minimum(
        jnp.searchsorted(pad_end, jnp.arange(n_blk, dtype=jnp.int32) * MOE_BLOCK, side='right'),
        N_EXPERTS - 1).astype(jnp.int32)
    n_used = (pad_end[-1:] // MOE_BLOCK).astype(jnp.int32)

    xs = _gather_rows(x, src_tok, MOE_BLOCK)
    ys = _experts(xs, block_expert, n_used, w_gate, w_up, w_down)
    return _combine_ln(ys, dest, slab, x, ln_g[None, :], ln_b[None, :])


def _rwkv_mixer(x_bf, w_in, mu, w0, w2, a0, a2, g2, k_k, k_a, r_k, gn_g, gn_b, batch, seq):
    c3 = 3 * MIX_WIDTH
    lo_w, lo_a, lo_g = c3, c3 + DECAY_LORA, c3 + DECAY_LORA + AAA_LORA
    pad = LANES - DECAY_LORA

    def permute_cols(t, with_mem):
        zeros = jnp.zeros(t.shape[:-1] + (pad,), t.dtype)
        parts = [t[..., :c3]]
        if with_mem:
            parts.append(t[..., RWKV_COLS:])
        parts += [t[..., lo_w:lo_a], zeros, t[..., lo_a:lo_g], zeros, t[..., lo_g:RWKV_COLS]]
        return jnp.concatenate(parts, axis=-1)

    w_perm = permute_cols(w_in, True).astype(BF16)
    mu_perm = permute_cols(mu[None, :], False)
    mu_main, mu_lora = mu_perm[:, :c3], mu_perm[:, c3:]
    z = _matmul(x_bf, w_perm, BF16, "rwkv_in_proj")

    zrow = jnp.zeros((pad, MIX_WIDTH), F32)
    w2p = jnp.concatenate([w2, zrow], axis=0).astype(BF16)
    a2p = jnp.concatenate([a2, zrow], axis=0).astype(BF16)
    r, k, v, an, bb, lw, gate, bonus = _rwkv_prep(
        z, mu_main, mu_lora, w2p, a2p, g2.astype(BF16), w0[None, :], a0[None, :], k_k[None, :], k_a[None, :],
        r_k.reshape(1, MIX_WIDTH), batch, seq)
    qw, y0, e, f, pc = _wkv_terms(r, k, v, an, bb, lw, batch, seq)
    mix = _wkv_scan(qw, y0, e, f, pc, bonus, gate, gn_g[None, :], gn_b[None, :], batch, seq)
    return z, mix, c3 // COL_BLOCK


def kernel(x, mem, w_out, mem_wk, mem_wv, ln_mix_g, ln_mix_b, ln_ffn_g, ln_ffn_b, router_group_w, router_group_b, router_expert_w, router_expert_b, expert_w_gate, expert_w_up, expert_w_down, rwkv_w_in, rwkv_mu, rwkv_w0, rwkv_w2, rwkv_a0, rwkv_a2, rwkv_g2, rwkv_k_k, rwkv_k_a, rwkv_r_k, rwkv_gn_g, rwkv_gn_b, conv_w_in, conv_w, sb_w_in):
    batch, seq, d = x.shape
    assert d == D_MODEL and mem.shape == (batch, MEM_LEN, D_MODEL)
    n = batch * seq
    xf = x.reshape(n, D_MODEL)
    xb = xf.astype(BF16)
    mem_bf = mem.reshape(batch * MEM_LEN, D_MODEL).astype(BF16)
    q_block = 3 * MIX_WIDTH // COL_BLOCK

    for i in range(DEPTH):
        kind, j = i % N_MIXERS, i // N_MIXERS
        if kind == 0:
            z, mix, _ = _rwkv_mixer(xb, rwkv_w_in[j], rwkv_mu[j], rwkv_w0[j], rwkv_w2[j], rwkv_a0[j], rwkv_a2[j],
                                    rwkv_g2[j], rwkv_k_k[j], rwkv_k_a[j], rwkv_r_k[j], rwkv_gn_g[j], rwkv_gn_b[j],
                                    batch, seq)
        elif kind == 1:
            z = _matmul(xb, conv_w_in[j].astype(BF16), BF16, "conv_in_proj")
            mix = _short_conv(z, conv_w[j], batch, seq)
        else:
            z = _matmul(xb, sb_w_in[j].astype(BF16), BF16, "sb_in_proj")
            mix = _stick_breaking(z, batch, seq)
        w_kv = jnp.concatenate([mem_wk[i], mem_wv[i]], axis=1).astype(BF16)
        kv = _matmul(mem_bf, w_kv, BF16, "mem_kv_proj")
        mem_out = _mem_attention(z, kv, batch, seq, q_block)
        w_o = w_out[i].astype(BF16)
        xf, xb = _outproj_ln(mix, mem_out, w_o[:MIX_WIDTH], w_o[MIX_WIDTH:], xf,
                             ln_mix_g[i][None, :], ln_mix_b[i][None, :])
        xf, xb = _moe_ln(xf, router_group_w[i], router_group_b[i], router_expert_w[i], router_expert_b[i],
                         expert_w_gate[i], expert_w_up[i], expert_w_down[i], ln_ffn_g[i], ln_ffn_b[i])
    return xf.reshape(batch, seq, D_MODEL)
```

```python
import functools
import math

import jax
import jax.numpy as jnp
from jax import lax
from jax.experimental import pallas as pl
from jax.experimental.pallas import tpu as pltpu

F32 = jnp.float32
BF16 = jnp.bfloat16

D_MODEL = 2048
DEPTH = 4
N_MIXERS = 3
MEM_LEN = 256
MIX_WIDTH = 1536
MEM_HEADS = 4
MEM_HEAD_DIM = 128
MEM_WIDTH = 512
RWKV_HEAD_DIM = 64
RWKV_HEADS = 24
DECAY_LORA = 96
AAA_LORA = 96
GATE_LORA = 256
RWKV_GN_EPS = 64e-5
RWKV_COLS = 3 * MIX_WIDTH + DECAY_LORA + AAA_LORA + GATE_LORA
SB_HEAD_DIM = 128
SB_HEADS = 12
N_GROUPS = 4
EXPERTS_PER_GROUP = 8
N_EXPERTS = 32
TOP_K = 2
D_EXPERT = 512
DEEPNORM_ALPHA = (2 * DEPTH) ** 0.25
LN_EPS = 1e-5

LANES = 128
V7X_VMEM_LIMIT_BYTES = 56 << 20

COL_BLOCK = 512
MOE_BLOCK = 256
PLAN_BLOCK = 512
GATHER_UNROLL = 8
TOKEN_TILE_ROWS = D_MODEL // LANES
WKV_CHUNK = 64
WKV_PAIR = 2 * RWKV_HEAD_DIM
WKV_CHUNKS_PER_STEP = 8
SB_Q_BLOCK = 256


def _params(*sem):
    return pltpu.CompilerParams(dimension_semantics=sem, vmem_limit_bytes=V7X_VMEM_LIMIT_BYTES)


def _dot(a, b):
    return jnp.dot(a, b, preferred_element_type=F32)


def _dot_nt(a, b):
    return lax.dot_general(a, b, (((1,), (1,)), ((), ())), preferred_element_type=F32)


def _dot_tn(a, b):
    return lax.dot_general(a, b, (((0,), (0,)), ((), ())), preferred_element_type=F32)


def _split2(x):
    hi = x.astype(BF16)
    lo = (x - hi.astype(F32)).astype(BF16)
    return hi, lo


def _dot_hl(x, w):
    hi, lo = _split2(x)
    return _dot(hi, w) + _dot(lo, w)


def _pick_tile(n, target, quantum=LANES):
    best = None
    for t in range(quantum, min(n, target) + 1, quantum):
        if n % t == 0:
            best = t
    assert best is not None, (n, target)
    return best


def _layer_norm_rows(y, g, b):
    mu = jnp.mean(y, axis=-1, keepdims=True)
    yc = y - mu
    var = jnp.mean(yc * yc, axis=-1, keepdims=True)
    return yc * lax.rsqrt(var + LN_EPS) * g + b


def _matmul_kernel(x_ref, w_ref, o_ref):
    o_ref[...] = _dot(x_ref[...], w_ref[...]).astype(o_ref.dtype)


def _matmul(x, w, out_dtype, name):
    m, k = x.shape
    n = w.shape[1]
    tm = _pick_tile(m, 1024, 8)
    tn = _pick_tile(n, 1536)
    return pl.pallas_call(
        _matmul_kernel,
        grid=(n // tn, m // tm),
        in_specs=[pl.BlockSpec((tm, k), lambda j, i: (i, 0)),
                  pl.BlockSpec((k, tn), lambda j, i: (0, j))],
        out_specs=pl.BlockSpec((tm, tn), lambda j, i: (i, j)),
        out_shape=jax.ShapeDtypeStruct((m, n), out_dtype),
        compiler_params=_params("parallel", "arbitrary"),
        name=name,
    )(x, w)


def _mem_attn_kernel(q_ref, kv_ref, o_ref):
    scale = 1.0 / math.sqrt(MEM_HEAD_DIM)
    for h in range(MEM_HEADS):
        sl = slice(h * MEM_HEAD_DIM, (h + 1) * MEM_HEAD_DIM)
        q = q_ref[:, sl]
        km = kv_ref[:, sl]
        vm = kv_ref[:, MEM_WIDTH + h * MEM_HEAD_DIM:MEM_WIDTH + (h + 1) * MEM_HEAD_DIM]
        s = _dot_nt(q, km) * scale
        s = s - jnp.max(s, axis=-1, keepdims=True)
        p = jnp.exp(s)
        p = p / jnp.sum(p, axis=-1, keepdims=True)
        o_ref[:, sl] = _dot(p.astype(BF16), vm).astype(o_ref.dtype)


def _mem_attention(z, kv, batch, seq, q_col_block):
    tq = _pick_tile(seq, 512, 8)
    nt = seq // tq
    return pl.pallas_call(
        _mem_attn_kernel,
        grid=(batch, nt),
        in_specs=[pl.BlockSpec((tq, MEM_WIDTH), lambda b, i: (b * nt + i, q_col_block)),
                  pl.BlockSpec((MEM_LEN, 2 * MEM_WIDTH), lambda b, i: (b, 0))],
        out_specs=pl.BlockSpec((tq, MEM_WIDTH), lambda b, i: (b * nt + i, 0)),
        out_shape=jax.ShapeDtypeStruct((batch * seq, MEM_WIDTH), BF16),
        compiler_params=_params("parallel", "parallel"),
        name="mem_attention",
    )(z, kv)


def _outproj_ln_kernel(mix_ref, mem_ref, w1_ref, w2_ref, x_ref, g_ref, b_ref, xo_ref, xt_ref):
    h = _dot(mix_ref[...], w1_ref[...]) + _dot(mem_ref[...], w2_ref[...])
    y = DEEPNORM_ALPHA * x_ref[...] + h
    out = _layer_norm_rows(y, g_ref[...], b_ref[...])
    xo_ref[...] = out
    _to_token_tiles(xt_ref, out)


def _outproj_ln(mix, mem_out, w_mix, w_mem, x, g, b):
    n = x.shape[0]
    tm = _pick_tile(n, 512, 8)
    row = lambda i: (i, 0)
    fixed = lambda i: (0, 0)
    return pl.pallas_call(
        _outproj_ln_kernel,
        grid=(n // tm,),
        in_specs=[pl.BlockSpec((tm, MIX_WIDTH), row),
                  pl.BlockSpec((tm, MEM_WIDTH), row),
                  pl.BlockSpec((MIX_WIDTH, D_MODEL), fixed),
                  pl.BlockSpec((MEM_WIDTH, D_MODEL), fixed),
                  pl.BlockSpec((tm, D_MODEL), row),
                  pl.BlockSpec((1, D_MODEL), fixed),
                  pl.BlockSpec((1, D_MODEL), fixed)],
        out_specs=[pl.BlockSpec((tm, D_MODEL), row), pl.BlockSpec((tm * TOKEN_TILE_ROWS, LANES), row)],
        out_shape=[jax.ShapeDtypeStruct((n, D_MODEL), F32),
                   jax.ShapeDtypeStruct((n * TOKEN_TILE_ROWS, LANES), F32)],
        compiler_params=_params("parallel"),
        name="outproj_ln",
    )(mix, mem_out, w_mix, w_mem, x, g, b)


def _conv_kernel(b_ref, c_ref, h_ref, w_ref, o_ref, carry_ref):
    @pl.when(pl.program_id(1) == 0)
    def _():
        carry_ref[...] = jnp.zeros_like(carry_ref)

    u = c_ref[...].astype(F32) * h_ref[...].astype(F32)
    tt = u.shape[0]
    rowid = lax.broadcasted_iota(jnp.int32, u.shape, 0)
    c0 = carry_ref[0:1, :]
    c1 = carry_ref[1:2, :]
    u1 = jnp.where(rowid == 0, c1, pltpu.roll(u, 1, axis=0))
    u2 = jnp.where(rowid == 0, c0, jnp.where(rowid == 1, c1, pltpu.roll(u, 2, axis=0)))
    carry_ref[0:2, :] = u[tt - 2:tt, :]
    y = w_ref[0:1, :] * u2 + w_ref[1:2, :] * u1 + w_ref[2:3, :] * u
    o_ref[...] = (b_ref[...].astype(F32) * y).astype(o_ref.dtype)


def _short_conv(z, conv_w, batch, seq):
    tt = _pick_tile(seq, 512, 8)
    nt = seq // tt
    spec = lambda c: pl.BlockSpec((tt, MIX_WIDTH), lambda b, i: (b * nt + i, c))
    return pl.pallas_call(
        _conv_kernel,
        grid=(batch, nt),
        in_specs=[spec(0), spec(1), spec(2),
                  pl.BlockSpec((3, MIX_WIDTH), lambda b, i: (0, 0))],
        out_specs=pl.BlockSpec((tt, MIX_WIDTH), lambda b, i: (b * nt + i, 0)),
        out_shape=jax.ShapeDtypeStruct((batch * seq, MIX_WIDTH), BF16),
        scratch_shapes=[pltpu.VMEM((8, MIX_WIDTH), F32)],
        compiler_params=_params("parallel", "arbitrary"),
        name="short_conv",
    )(z, z, z, conv_w)


def _sb_kernel(q_ref, k_ref, v_ref, o_ref):
    i = pl.program_id(2)
    tq = q_ref.shape[0]
    bk = tq
    scale = 1.0 / math.sqrt(SB_HEAD_DIM)
    q = q_ref[...]
    r = lax.broadcasted_iota(jnp.int32, (2 * bk, bk), 0)
    c = lax.broadcasted_iota(jnp.int32, (2 * bk, bk), 1)
    suffix = jnp.where((r & (bk - 1)) > c, 1.0, 0.0).astype(BF16)

    def rows_of(ref, j):
        return ref[pl.ds(pl.multiple_of(j * bk, bk), bk), :]

    def scores(j):
        return _dot_nt(q, rows_of(k_ref, j)) * scale

    def logs(s, valid):
        soft = jnp.log(1.0 + jnp.exp(-jnp.abs(s)))
        log_beta = jnp.minimum(s, 0.0) - soft
        log_not = jnp.minimum(-s, 0.0) - soft
        if valid is not None:
            log_not = jnp.where(valid, log_not, 0.0)
        hi, lo = _split2(log_not)
        return log_beta, jnp.concatenate([hi, lo], axis=1), jnp.sum(log_not, axis=-1, keepdims=True)

    def weights(log_beta, within, later, valid):
        attn = jnp.exp(log_beta + within + later)
        if valid is not None:
            attn = jnp.where(valid, attn, 0.0)
        return attn.astype(BF16)

    def blocks(js, later, valid):
        s = [scores(j) for j in js]
        lg = [logs(x, valid) for x in s]
        within = [_dot(x[1], suffix) for x in lg]
        pv = None
        for j, (log_beta, _, total), w in zip(js, lg, within):
            p = _dot(weights(log_beta, w, later, valid), rows_of(v_ref, j))
            pv = p if pv is None else pv + p
            later = later + total
        return pv, later

    valid = lax.broadcasted_iota(jnp.int32, (tq, bk), 1) < lax.broadcasted_iota(jnp.int32, (tq, bk), 0)
    acc, later = blocks([i], jnp.zeros((tq, 1), F32), valid)

    def single(step, carry):
        acc, later = carry
        pv, later = blocks([i - 1], later, None)
        return acc + pv, later

    def pair(step, carry):
        acc, later = carry
        j = i - 1 - odd - 2 * step
        pv, later = blocks([j, j - 1], later, None)
        return acc + pv, later

    odd = i & 1
    carry = lax.fori_loop(0, odd, single, (acc, later))
    acc, _ = lax.fori_loop(0, lax.shift_right_logical(i, 1), pair, carry)
    o_ref[...] = acc.astype(o_ref.dtype)


def _stick_breaking(z, batch, seq):
    tq = _pick_tile(seq, SB_Q_BLOCK, LANES)
    nq = seq // tq
    return pl.pallas_call(
        _sb_kernel,
        grid=(batch, SB_HEADS, nq),
        in_specs=[pl.BlockSpec((tq, SB_HEAD_DIM), lambda b, h, i: (b * nq + i, h)),
                  pl.BlockSpec((seq, SB_HEAD_DIM), lambda b, h, i: (b, SB_HEADS + h)),
                  pl.BlockSpec((seq, SB_HEAD_DIM), lambda b, h, i: (b, 2 * SB_HEADS + h))],
        out_specs=pl.BlockSpec((tq, SB_HEAD_DIM), lambda b, h, i: (b * nq + i, h)),
        out_shape=jax.ShapeDtypeStruct((batch * seq, MIX_WIDTH), BF16),
        compiler_params=_params("parallel", "parallel", "arbitrary"),
        name="stick_breaking",
    )(z, z, z)


def _rwkv_prep_kernel(zr_ref, zk_ref, zv_ref, zl_ref, mur_ref, muk_ref, muv_ref, mul_ref,
                      w2_ref, a2_ref, g2_ref, w0_ref, a0_ref, kk_ref, ka_ref, rk_ref,
                      seg_ref, segt_ref,
                      r_out, k_out, v_out, an_out, bb_out, lw_out, gate_out, bonus_out,
                      carry_ref):
    @pl.when(pl.program_id(2) == 0)
    def _():
        carry_ref[...] = jnp.zeros_like(carry_ref)

    def shifted(z_ref, slot, mu_ref):
        z = z_ref[...].astype(F32)
        tt = z.shape[0]
        rowid = lax.broadcasted_iota(jnp.int32, z.shape, 0)
        prev = jnp.where(rowid == 0, carry_ref[slot:slot + 1, :], pltpu.roll(z, 1, axis=0))
        carry_ref[slot:slot + 1, :] = z[tt - 1:tt, :]
        return z + (prev - z) * mu_ref[...]

    r = shifted(zr_ref, 0, mur_ref)
    k = shifted(zk_ref, 1, muk_ref)
    v = shifted(zv_ref, 2, muv_ref)
    zl = shifted(zl_ref, 3, mul_ref)
    zw = zl[:, 0:LANES]
    za = zl[:, LANES:2 * LANES]
    zg = zl[:, 2 * LANES:4 * LANES]

    dw = w0_ref[...] + _dot(jnp.tanh(zw).astype(BF16), w2_ref[...])
    neg = -dw
    softplus = jnp.maximum(neg, 0.0) + jnp.log(1.0 + jnp.exp(-jnp.abs(neg)))
    lw = -jnp.exp(-softplus - 0.5)
    a = jax.nn.sigmoid(a0_ref[...] + _dot(za.astype(BF16), a2_ref[...]))
    gate = _dot(jax.nn.sigmoid(zg).astype(BF16), g2_ref[...])

    seg = seg_ref[...]
    segt = segt_ref[...]
    kk = k * kk_ref[...]
    norm = jnp.sqrt(_dot_hl(_dot_hl(kk * kk, seg), segt))
    kkn = kk / jnp.maximum(norm, 1e-12)
    k2 = k * (1.0 + (a - 1.0) * ka_ref[...])
    bonus = _dot_hl(_dot_hl(r * k2 * rk_ref[...], seg), segt) * v

    r_out[...] = r
    k_out[...] = k2
    v_out[...] = v
    an_out[...] = -kkn
    bb_out[...] = kkn * a
    lw_out[...] = lw
    gate_out[...] = gate
    bonus_out[...] = bonus


def _rwkv_prep(z, mu_main, mu_lora, w2p, a2p, g2, w0, a0, k_k, k_a, r_k, batch, seq):
    n = batch * seq
    tt = _pick_tile(seq, 256, 8)
    nt = seq // tt
    nc = MIX_WIDTH // COL_BLOCK
    heads_per_block = COL_BLOCK // RWKV_HEAD_DIM
    ch = jnp.arange(COL_BLOCK)[:, None] // RWKV_HEAD_DIM
    seg = (ch == jnp.arange(LANES)[None, :]).astype(BF16)
    segt = seg.T
    assert heads_per_block <= LANES

    zspec = lambda off: pl.BlockSpec((tt, COL_BLOCK), lambda c, b, i: (b * nt + i, off + c))
    zlspec = pl.BlockSpec((tt, COL_BLOCK), lambda c, b, i: (b * nt + i, 3 * nc + 1))
    muspec = lambda off: pl.BlockSpec((1, COL_BLOCK), lambda c, b, i: (0, off + c))
    colspec = lambda rows: pl.BlockSpec((rows, COL_BLOCK), lambda c, b, i: (0, c))
    fixed = lambda shape: pl.BlockSpec(shape, lambda c, b, i: (0, 0))
    ospec = pl.BlockSpec((tt, COL_BLOCK), lambda c, b, i: (b * nt + i, c))
    outs = pl.pallas_call(
        _rwkv_prep_kernel,
        grid=(nc, batch, nt),
        in_specs=[zspec(0), zspec(nc), zspec(2 * nc), zlspec,
                  muspec(0), muspec(nc), muspec(2 * nc), fixed((1, COL_BLOCK)),
                  colspec(LANES), colspec(LANES), colspec(GATE_LORA),
                  colspec(1), colspec(1), colspec(1), colspec(1), colspec(1),
                  fixed((COL_BLOCK, LANES)), fixed((LANES, COL_BLOCK))],
        out_specs=[ospec] * 8,
        out_shape=[jax.ShapeDtypeStruct((n, MIX_WIDTH), F32)] * 8,
        scratch_shapes=[pltpu.VMEM((8, COL_BLOCK), F32)],
        compiler_params=_params("parallel", "parallel", "arbitrary"),
        name="rwkv_prep",
    )(z, z, z, z, mu_main, mu_main, mu_main, mu_lora, w2p, a2p, g2, w0, a0, k_k, k_a, r_k, seg, segt)
    return outs


def _each(fn, *cols):
    return [fn(*args) for args in zip(*cols)]


def _wkv_chunk_terms(chunks, consts):
    tri, lane_lo, strict, incl, blk16, eye = consts
    c = WKV_CHUNK
    n2 = 2 * c

    def pair_rows(x):
        return jnp.concatenate([jnp.where(lane_lo, x, 0.0), jnp.where(lane_lo, 0.0, x)], axis=0)

    def mm(x, y):
        return _dot(x.astype(BF16), y.astype(BF16))

    def cumulative(r, k, v, an, bb, lw):
        h1 = lw.astype(BF16)
        r1 = lw - h1.astype(F32)
        h2 = r1.astype(BF16)
        h3 = (r1 - h2.astype(F32)).astype(BF16)
        return _dot(tri, h1) + _dot(tri, h2) + _dot(tri, h3)

    cums = _each(cumulative, *zip(*chunks))

    def operands(chunk, cum):
        r, k, v, an, bb, lw = chunk
        cum_last = cum[c - 1:c, :]
        p_inv = jnp.exp(-cum)
        p_tail = jnp.exp(cum_last - cum)
        a_p = pair_rows(an * jnp.exp(cum - lw)).astype(BF16)
        q_p = pair_rows(r * jnp.exp(cum))
        b_p = pair_rows(bb * p_inv).astype(BF16)
        k_p = pair_rows(k * p_inv).astype(BF16)
        v_p = pair_rows(v).astype(BF16)
        bt_p = pair_rows(bb * p_tail).astype(BF16)
        kt_p = pair_rows(k * p_tail).astype(BF16)
        aq = jnp.concatenate([a_p, q_p.astype(BF16)], axis=0)
        bk = jnp.concatenate([b_p, k_p], axis=0)
        return a_p, q_p, v_p, jnp.concatenate([bt_p, kt_p], axis=0), aq, bk, jnp.exp(cum_last)

    a_p, q_p, v_p, btkt, aq, bk, pc = zip(*_each(operands, chunks, cums))
    g = _each(_dot_nt, aq, bk)
    l_ab = [jnp.where(strict, x[:n2, :n2], 0.0) for x in g]
    a_rb = [jnp.where(incl, x[n2:, :n2], 0.0).astype(BF16) for x in g]
    a_kk = [jnp.concatenate([jnp.where(strict, x[:n2, n2:], 0.0), jnp.where(incl, x[n2:, n2:], 0.0)],
                            axis=0).astype(BF16) for x in g]
    av = _each(_dot, a_kk, v_p)

    ld = [jnp.where(blk16, x, 0.0) for x in l_ab]
    lo = [x - y for x, y in zip(l_ab, ld)]
    l2 = _each(mm, ld, ld)
    l4 = _each(mm, l2, l2)
    l8 = _each(mm, l4, l4)
    dinv = _each(mm, [eye + x for x in ld], [eye + x for x in l2])
    dinv = _each(mm, dinv, [eye + x for x in l4])
    dinv = _each(mm, dinv, [eye + x for x in l8])
    m1 = _each(mm, dinv, lo)
    m2 = _each(mm, m1, m1)
    tinv = _each(mm, [eye + x for x in m1], [eye + x for x in m2])
    tinv = _each(mm, tinv, dinv)

    wu = _each(mm, tinv, [jnp.concatenate([a, x[:n2].astype(BF16)], axis=1) for a, x in zip(a_p, av)])
    wu_b = [x.astype(BF16) for x in wu]
    rb = _each(_dot, a_rb, wu_b)
    qw = [q + x[:, :n2] for q, x in zip(q_p, rb)]
    y0 = [x[:, n2:] + y[n2:] for x, y in zip(rb, av)]
    e = _each(_dot_tn, [x[:, :n2] for x in wu_b], [x[:n2] for x in btkt])
    f = _each(_dot_tn, [jnp.concatenate([x[:, n2:], y], axis=0) for x, y in zip(wu_b, v_p)], btkt)
    return [(a[:c] + a[c:], b[:c] + b[c:], ee, ff, p) for a, b, ee, ff, p in zip(qw, y0, e, f, pc)]


def _wkv_terms_kernel(r_ref, k_ref, v_ref, an_ref, bb_ref, lw_ref, qw_ref, y0_ref, e_ref, f_ref, pc_ref):
    c = WKV_CHUNK
    n2 = 2 * c
    row = lax.broadcasted_iota(jnp.int32, (n2, n2), 0)
    col = lax.broadcasted_iota(jnp.int32, (n2, n2), 1)
    tri_r = lax.broadcasted_iota(jnp.int32, (c, c), 0)
    tri_c = lax.broadcasted_iota(jnp.int32, (c, c), 1)
    consts = (
        jnp.where(tri_c <= tri_r, 1.0, 0.0).astype(BF16),
        lax.broadcasted_iota(jnp.int32, (c, WKV_PAIR), 1) < RWKV_HEAD_DIM,
        row > col,
        row >= col,
        (row >> 4) == (col >> 4),
        jnp.where(row == col, 1.0, 0.0),
    )
    n_chunks = r_ref.shape[0] // c
    chunks = []
    for j in range(n_chunks):
        rows = slice(j * c, (j + 1) * c)
        chunks.append(tuple(ref[rows, :] for ref in (r_ref, k_ref, v_ref, an_ref, bb_ref, lw_ref)))
    for j, (qw, y0, e, f, pc) in enumerate(_wkv_chunk_terms(chunks, consts)):
        rows = slice(j * c, (j + 1) * c)
        qw_ref[rows, :] = qw.astype(qw_ref.dtype)
        y0_ref[rows, :] = y0
        e_ref[j * n2:(j + 1) * n2, :] = e.astype(e_ref.dtype)
        f_ref[j * n2:(j + 1) * n2, :] = f
        pc_ref[j * 8:(j + 1) * 8, :] = jnp.broadcast_to(pc, (8, WKV_PAIR))


def _wkv_terms(r, k, v, an, bb, lw, batch, seq):
    n = batch * seq
    c = WKV_CHUNK
    n_pairs = MIX_WIDTH // WKV_PAIR
    cps = _pick_tile(seq // c, WKV_CHUNKS_PER_STEP, 1)
    steps = seq // (c * cps)
    tok = pl.BlockSpec((cps * c, WKV_PAIR), lambda b, p, i: (b * steps + i, p))
    mat = pl.BlockSpec((cps * WKV_PAIR, WKV_PAIR), lambda b, p, i: (b * steps + i, p))
    vec = pl.BlockSpec((cps * 8, WKV_PAIR), lambda b, p, i: (b * steps + i, p))
    n_chunks = n // c
    return pl.pallas_call(
        _wkv_terms_kernel,
        grid=(batch, n_pairs, steps),
        in_specs=[tok] * 6,
        out_specs=[tok, tok, mat, mat, vec],
        out_shape=[jax.ShapeDtypeStruct((n, MIX_WIDTH), BF16),
                   jax.ShapeDtypeStruct((n, MIX_WIDTH), F32),
                   jax.ShapeDtypeStruct((n_chunks * WKV_PAIR, MIX_WIDTH), BF16),
                   jax.ShapeDtypeStruct((n_chunks * WKV_PAIR, MIX_WIDTH), F32),
                   jax.ShapeDtypeStruct((n_chunks * 8, MIX_WIDTH), F32)],
        compiler_params=_params("parallel", "parallel", "parallel"),
        name="wkv_terms",
    )(r, k, v, an, bb, lw)


def _wkv_scan_kernel(qw_ref, y0_ref, e_ref, f_ref, pc_ref, bonus_ref, gate_ref, gng_ref, gnb_ref,
                     o_ref, state_ref):
    @pl.when(pl.program_id(2) == 0)
    def _():
        state_ref[...] = jnp.zeros_like(state_ref)

    n_pairs = state_ref.shape[1] // WKV_PAIR
    lane_lo = lax.broadcasted_iota(jnp.int32, (WKV_CHUNK, WKV_PAIR), 1) < RWKV_HEAD_DIM
    inv_n = 1.0 / RWKV_HEAD_DIM
    for p in range(n_pairs):
        sl = slice(p * WKV_PAIR, (p + 1) * WKV_PAIR)
        s = state_ref[:, sl]
        sb = s.astype(BF16)
        y = _dot_nt(qw_ref[:, sl], sb) + y0_ref[:, sl]
        state_ref[:, sl] = s * pc_ref[0:1, sl] + _dot(sb, e_ref[:, sl]) + f_ref[:, sl]

        def head_mean(x):
            lo = jnp.sum(jnp.where(lane_lo, x, 0.0), axis=-1, keepdims=True)
            hi = jnp.sum(jnp.where(lane_lo, 0.0, x), axis=-1, keepdims=True)
            return jnp.where(lane_lo, lo, hi) * inv_n

        yc = y - head_mean(y)
        var = head_mean(yc * yc)
        yn = yc * lax.rsqrt(var + RWKV_GN_EPS) * gng_ref[:, sl] + gnb_ref[:, sl]
        o_ref[:, sl] = ((yn + bonus_ref[:, sl]) * gate_ref[:, sl]).astype(o_ref.dtype)


def _wkv_scan(qw, y0, e, f, pc, bonus, gate, gn_g, gn_b, batch, seq):
    n = batch * seq
    c = WKV_CHUNK
    nc = seq // c
    width = MIX_WIDTH // 2
    groups = MIX_WIDTH // width
    tok = pl.BlockSpec((c, width), lambda b, g, i: (b * nc + i, g))
    mat = pl.BlockSpec((WKV_PAIR, width), lambda b, g, i: (b * nc + i, g))
    vec = pl.BlockSpec((8, width), lambda b, g, i: (b * nc + i, g))
    par = pl.BlockSpec((1, width), lambda b, g, i: (0, g))
    return pl.pallas_call(
        _wkv_scan_kernel,
        grid=(batch, groups, nc),
        in_specs=[tok, tok, mat, mat, vec, tok, tok, par, par],
        out_specs=tok,
        out_shape=jax.ShapeDtypeStruct((n, MIX_WIDTH), BF16),
        scratch_shapes=[pltpu.VMEM((WKV_PAIR, width), F32)],
        compiler_params=_params("parallel", "parallel", "arbitrary"),
        name="wkv_scan",
    )(qw, y0, e, f, pc, bonus, gate, gn_g, gn_b)


def _router_kernel(x_ref, w_ref, b_ref, o_ref):
    logits = jnp.dot(x_ref[...], w_ref[...], preferred_element_type=F32,
                     precision=lax.Precision.HIGHEST) + b_ref[...]
    lane = lax.broadcasted_iota(jnp.int32, logits.shape, 1)
    neg_inf = -jnp.inf
    big = 4 * LANES

    def first_where(mask):
        return jnp.min(jnp.where(mask, lane, big), axis=-1, keepdims=True)

    gl = jnp.where(lane < N_GROUPS, logits, neg_inf)
    gmax = jnp.max(gl, axis=-1, keepdims=True)
    gidx = first_where(gl == gmax)
    p_group_sel = 1.0 / jnp.sum(jnp.exp(gl - gmax), axis=-1, keepdims=True)

    lo = N_GROUPS + EXPERTS_PER_GROUP * gidx
    emask = (lane >= lo) & (lane < lo + EXPERTS_PER_GROUP)
    el = jnp.where(emask, logits, neg_inf)
    emax = jnp.max(el, axis=-1, keepdims=True)
    pe = jnp.exp(el - emax)
    pe = pe / jnp.sum(pe, axis=-1, keepdims=True)
    pe = jnp.where(emask, pe, -1.0)
    p1 = jnp.max(pe, axis=-1, keepdims=True)
    i1 = first_where(pe == p1)
    pe2 = jnp.where(lane == i1, -1.0, pe)
    p2 = jnp.max(pe2, axis=-1, keepdims=True)
    i2 = first_where(pe2 == p2)
    denom = p1 + p2
    g1 = p_group_sel * p1 / denom
    g2 = p_group_sel * p2 / denom
    e1 = (i1 - N_GROUPS).astype(F32)
    e2 = (i2 - N_GROUPS).astype(F32)
    o_ref[...] = jnp.where(lane == 0, e1, jnp.where(lane == 1, e2, jnp.where(lane == 2, g1, jnp.where(lane == 3, g2, 0.0))))


def _router(x, w_router, b_router):
    n = x.shape[0]
    tm = _pick_tile(n, 512, 8)
    return pl.pallas_call(
        _router_kernel,
        grid=(n // tm,),
        in_specs=[pl.BlockSpec((tm, D_MODEL), lambda i: (i, 0)),
                  pl.BlockSpec((D_MODEL, LANES), lambda i: (0, 0)),
                  pl.BlockSpec((1, LANES), lambda i: (0, 0))],
        out_specs=pl.BlockSpec((tm, LANES), lambda i: (i, 0)),
        out_shape=jax.ShapeDtypeStruct((n, LANES), F32),
        compiler_params=_params("parallel"),
        name="moe_router",
    )(x, w_router, b_router)


def _plan_kernel(e_ref, rank_ref, count_ref, carry_ref):
    @pl.when(pl.program_id(0) == 0)
    def _():
        carry_ref[...] = jnp.zeros_like(carry_ref)

    nb = e_ref.shape[-1]
    e = e_ref[0]
    onehot = jnp.where(lax.broadcasted_iota(jnp.int32, (N_EXPERTS, nb), 0) == e, 1.0, 0.0)
    r = lax.broadcasted_iota(jnp.int32, (nb, nb), 0)
    c = lax.broadcasted_iota(jnp.int32, (nb, nb), 1)
    before = jnp.where(r < c, 1.0, 0.0).astype(BF16)
    seen = _dot(onehot.astype(BF16), before) + carry_ref[:, 0:1]
    rank_ref[0] = jnp.sum(onehot * seen, axis=0, keepdims=True).astype(jnp.int32)
    carry_ref[...] = carry_ref[...] + jnp.sum(onehot, axis=1, keepdims=True)
    count_ref[...] = carry_ref[...]


def _dispatch_plan(e_flat):
    n_slots = e_flat.shape[0]
    nb = _pick_tile(n_slots, PLAN_BLOCK)
    steps = n_slots // nb
    rank, counts = pl.pallas_call(
        _plan_kernel,
        grid=(steps,),
        in_specs=[pl.BlockSpec((1, 1, nb), lambda i: (i, 0, 0))],
        out_specs=[pl.BlockSpec((1, 1, nb), lambda i: (i, 0, 0)),
                   pl.BlockSpec((N_EXPERTS, LANES), lambda i: (0, 0))],
        out_shape=[jax.ShapeDtypeStruct((steps, 1, nb), jnp.int32),
                   jax.ShapeDtypeStruct((N_EXPERTS, LANES), F32)],
        scratch_shapes=[pltpu.VMEM((N_EXPERTS, LANES), F32)],
        compiler_params=_params("arbitrary"),
        name="moe_plan",
    )(e_flat.reshape(steps, 1, nb))
    return rank.reshape(n_slots), counts[:, 0].astype(jnp.int32)


def _to_token_tiles(dst_ref, value):
    rows = value.shape[0]
    for j in range(TOKEN_TILE_ROWS):
        dst_ref[pl.ds(j, rows, stride=TOKEN_TILE_ROWS), :] = value[:, j * LANES:(j + 1) * LANES]


def _token_tile_columns(src_ref, rows, j):
    return src_ref[pl.ds(j, rows, stride=TOKEN_TILE_ROWS), :]


def _row_copy(table_hbm, dst_ref, src_row, dst_row, sem):
    src = pl.multiple_of(src_row * TOKEN_TILE_ROWS, TOKEN_TILE_ROWS)
    dst = pl.multiple_of(dst_row * TOKEN_TILE_ROWS, TOKEN_TILE_ROWS)
    return pltpu.make_async_copy(table_hbm.at[pl.ds(src, TOKEN_TILE_ROWS)],
                                 dst_ref.at[pl.ds(dst, TOKEN_TILE_ROWS)], sem)


def _start_row_gather(idx_ref, offset, stride, table_hbm, dst_ref, sem):
    def start(i, carry):
        _row_copy(table_hbm, dst_ref, idx_ref[0, 0, offset + stride * i], i, sem).start()
        return carry

    lax.fori_loop(0, dst_ref.shape[0] // TOKEN_TILE_ROWS, start, 0, unroll=GATHER_UNROLL)


def _wait_row_gather(table_hbm, dst_ref, sem):
    def wait(i, carry):
        _row_copy(table_hbm, dst_ref, 0, i, sem).wait()
        return carry

    lax.fori_loop(0, dst_ref.shape[0] // TOKEN_TILE_ROWS, wait, 0, unroll=GATHER_UNROLL)


def _gather_kernel(idx_ref, table_hbm, o_ref, sem):
    _start_row_gather(idx_ref, 0, 1, table_hbm, o_ref, sem.at[0])
    _wait_row_gather(table_hbm, o_ref, sem.at[0])


def _gather_rows(table, idx, tokens_per_step):
    n_out = idx.shape[0]
    steps = n_out // tokens_per_step
    return pl.pallas_call(
        _gather_kernel,
        grid=(steps,),
        in_specs=[pl.BlockSpec((1, 1, tokens_per_step), lambda i: (i, 0, 0), memory_space=pltpu.SMEM),
                  pl.BlockSpec(memory_space=pl.ANY)],
        out_specs=pl.BlockSpec((tokens_per_step * TOKEN_TILE_ROWS, LANES), lambda i: (i, 0)),
        out_shape=jax.ShapeDtypeStruct((n_out * TOKEN_TILE_ROWS, LANES), table.dtype),
        scratch_shapes=[pltpu.SemaphoreType.DMA((1,))],
        compiler_params=_params("arbitrary"),
        name="moe_dispatch_gather",
    )(idx.reshape(steps, 1, tokens_per_step), table)


def _expert_kernel(be_ref, nu_ref, x_ref, wg_ref, wu_ref, wd_ref, o_ref, wg_s, wu_s, wd_s, xb_s):
    i = pl.program_id(0)
    prev = be_ref[jnp.maximum(i - 1, 0)]
    changed = (i == 0) | (be_ref[i] != prev)

    @pl.when(changed)
    def _():
        wg_s[...] = wg_ref[0, 0].astype(BF16)
        wu_s[...] = wu_ref[0, 0].astype(BF16)
        wd_s[...] = wd_ref[0, 0].astype(BF16)

    used = i < nu_ref[0]

    @pl.when(used)
    def _():
        for j in range(TOKEN_TILE_ROWS):
            xb_s[:, j * LANES:(j + 1) * LANES] = _token_tile_columns(x_ref, MOE_BLOCK, j).astype(BF16)
        xb = xb_s[...]
        g = _dot(xb, wg_s[...])
        u = _dot(xb, wu_s[...])
        h = (g * jax.nn.sigmoid(g)) * u
        _to_token_tiles(o_ref, _dot(h.astype(BF16), wd_s[...]))

    @pl.when(jnp.logical_not(used))
    def _():
        o_ref[...] = jnp.zeros_like(o_ref)


def _experts(xs, block_expert, n_used, w_gate, w_up, w_down, layer):
    cap = xs.shape[0] // TOKEN_TILE_ROWS
    n_blk = cap // MOE_BLOCK
    wspec = lambda shape: pl.BlockSpec((1, 1) + shape, lambda i, be, nu: (layer, be[i], 0, 0))
    tiles = pl.BlockSpec((MOE_BLOCK * TOKEN_TILE_ROWS, LANES), lambda i, be, nu: (i, 0))
    grid_spec = pltpu.PrefetchScalarGridSpec(
        num_scalar_prefetch=2,
        grid=(n_blk,),
        in_specs=[tiles, wspec((D_MODEL, D_EXPERT)), wspec((D_MODEL, D_EXPERT)), wspec((D_EXPERT, D_MODEL))],
        out_specs=tiles,
        scratch_shapes=[pltpu.VMEM((D_MODEL, D_EXPERT), BF16),
                        pltpu.VMEM((D_MODEL, D_EXPERT), BF16),
                        pltpu.VMEM((D_EXPERT, D_MODEL), BF16),
                        pltpu.VMEM((MOE_BLOCK, D_MODEL), BF16)],
    )
    return pl.pallas_call(
        _expert_kernel,
        grid_spec=grid_spec,
        out_shape=jax.ShapeDtypeStruct((cap * TOKEN_TILE_ROWS, LANES), F32),
        compiler_params=_params("arbitrary"),
        name="moe_experts",
    )(block_expert, n_used, xs, w_gate, w_up, w_down)


def _combine_ln_kernel(idx_ref, ys_hbm, slab_ref, x_ref, g_ref, b_ref, xo_ref, xb_ref, buf_a, buf_b, y_s, sem):
    _start_row_gather(idx_ref, 0, 2, ys_hbm, buf_a, sem.at[0])
    _start_row_gather(idx_ref, 1, 2, ys_hbm, buf_b, sem.at[1])
    slab = slab_ref[...]
    lane = lax.broadcasted_iota(jnp.int32, slab.shape, 1)
    g1 = jnp.sum(jnp.where(lane == 2, slab, 0.0), axis=-1, keepdims=True)
    g2 = jnp.sum(jnp.where(lane == 3, slab, 0.0), axis=-1, keepdims=True)
    tm = x_ref.shape[0]
    _wait_row_gather(ys_hbm, buf_a, sem.at[0])
    _wait_row_gather(ys_hbm, buf_b, sem.at[1])
    for j in range(TOKEN_TILE_ROWS):
        cols = slice(j * LANES, (j + 1) * LANES)
        f = _token_tile_columns(buf_a, tm, j) * g1 + _token_tile_columns(buf_b, tm, j) * g2
        y_s[:, cols] = DEEPNORM_ALPHA * x_ref[:, cols] + f
    out = _layer_norm_rows(y_s[...], g_ref[...], b_ref[...])
    xo_ref[...] = out
    xb_ref[...] = out.astype(BF16)


def _combine_ln(ys, dest, slab, x, g, b):
    n = x.shape[0]
    tm = _pick_tile(n, 256, 8)
    steps = n // tm
    row = lambda i: (i, 0)
    fixed = lambda i: (0, 0)
    return pl.pallas_call(
        _combine_ln_kernel,
        grid=(steps,),
        in_specs=[pl.BlockSpec((1, 1, TOP_K * tm), lambda i: (i, 0, 0), memory_space=pltpu.SMEM),
                  pl.BlockSpec(memory_space=pl.ANY),
                  pl.BlockSpec((tm, LANES), row),
                  pl.BlockSpec((tm, D_MODEL), row),
                  pl.BlockSpec((1, D_MODEL), fixed),
                  pl.BlockSpec((1, D_MODEL), fixed)],
        out_specs=[pl.BlockSpec((tm, D_MODEL), row), pl.BlockSpec((tm, D_MODEL), row)],
        out_shape=[jax.ShapeDtypeStruct((n, D_MODEL), F32), jax.ShapeDtypeStruct((n, D_MODEL), BF16)],
        scratch_shapes=[pltpu.VMEM((tm * TOKEN_TILE_ROWS, LANES), F32),
                        pltpu.VMEM((tm * TOKEN_TILE_ROWS, LANES), F32),
                        pltpu.VMEM((tm, D_MODEL), F32),
                        pltpu.SemaphoreType.DMA((2,))],
        compiler_params=_params("arbitrary"),
        name="moe_combine_ln",
    )(dest.reshape(steps, 1, TOP_K * tm), ys, slab, x, g, b)


def _moe_ln(x, x_tiles, rg_w, rg_b, re_w, re_b, w_gate, w_up, w_down, layer, ln_g, ln_b):
    n = x.shape[0]
    n_slots = n * TOP_K
    pad_cols = LANES - N_GROUPS - N_EXPERTS
    w_router = jnp.concatenate([rg_w, re_w, jnp.zeros((D_MODEL, pad_cols), F32)], axis=1)
    b_router = jnp.concatenate([rg_b, re_b, jnp.zeros((pad_cols,), F32)])[None, :]
    slab = _router(x, w_router, b_router)

    e_flat = slab[:, :TOP_K].astype(jnp.int32).reshape(n_slots)
    rank, counts = _dispatch_plan(e_flat)
    padded = (counts + MOE_BLOCK - 1) // MOE_BLOCK * MOE_BLOCK
    pad_end = jnp.cumsum(padded)
    pad_start = pad_end - padded
    dest = pad_start[e_flat] + rank
    cap = n_slots + N_EXPERTS * MOE_BLOCK
    n_blk = cap // MOE_BLOCK
    src_tok = jnp.zeros((cap,), jnp.int32).at[dest].set(jnp.arange(n_slots, dtype=jnp.int32) // TOP_K)
    block_start = jnp.arange(n_blk, dtype=jnp.int32) * MOE_BLOCK
    block_expert = jnp.minimum(jnp.sum((pad_end[None, :] <= block_start[:, None]).astype(jnp.int32), axis=1),
                               N_EXPERTS - 1)
    n_used = (pad_end[-1:] // MOE_BLOCK).astype(jnp.int32)

    xs = _gather_rows(x_tiles, src_tok, MOE_BLOCK)
    ys = _experts(xs, block_expert, n_used, w_gate, w_up, w_down, layer)
    return _combine_ln(ys, dest, slab, x, ln_g[None, :], ln_b[None, :])


def _rwkv_mixer(x_bf, w_in, mu, w0, w2, a0, a2, g2, k_k, k_a, r_k, gn_g, gn_b, batch, seq):
    c3 = 3 * MIX_WIDTH
    lo_w, lo_a, lo_g = c3, c3 + DECAY_LORA, c3 + DECAY_LORA + AAA_LORA
    pad = LANES - DECAY_LORA

    def permute_cols(t, with_mem):
        zeros = jnp.zeros(t.shape[:-1] + (pad,), t.dtype)
        parts = [t[..., :c3]]
        if with_mem:
            parts.append(t[..., RWKV_COLS:])
        parts += [t[..., lo_w:lo_a], zeros, t[..., lo_a:lo_g], zeros, t[..., lo_g:RWKV_COLS]]
        return jnp.concatenate(parts, axis=-1)

    w_perm = permute_cols(w_in, True).astype(BF16)
    mu_perm = permute_cols(mu[None, :], False)
    mu_main, mu_lora = mu_perm[:, :c3], mu_perm[:, c3:]
    z = _matmul(x_bf, w_perm, BF16, "rwkv_in_proj")

    zrow = jnp.zeros((pad, MIX_WIDTH), F32)
    w2p = jnp.concatenate([w2, zrow], axis=0).astype(BF16)
    a2p = jnp.concatenate([a2, zrow], axis=0).astype(BF16)
    r, k, v, an, bb, lw, gate, bonus = _rwkv_prep(
        z, mu_main, mu_lora, w2p, a2p, g2.astype(BF16), w0[None, :], a0[None, :], k_k[None, :], k_a[None, :],
        r_k.reshape(1, MIX_WIDTH), batch, seq)
    qw, y0, e, f, pc = _wkv_terms(r, k, v, an, bb, lw, batch, seq)
    mix = _wkv_scan(qw, y0, e, f, pc, bonus, gate, gn_g[None, :], gn_b[None, :], batch, seq)
    return z, mix, c3 // COL_BLOCK


def kernel(x, mem, w_out, mem_wk, mem_wv, ln_mix_g, ln_mix_b, ln_ffn_g, ln_ffn_b, router_group_w, router_group_b, router_expert_w, router_expert_b, expert_w_gate, expert_w_up, expert_w_down, rwkv_w_in, rwkv_mu, rwkv_w0, rwkv_w2, rwkv_a0, rwkv_a2, rwkv_g2, rwkv_k_k, rwkv_k_a, rwkv_r_k, rwkv_gn_g, rwkv_gn_b, conv_w_in, conv_w, sb_w_in):
    batch, seq, d = x.shape
    assert d == D_MODEL and mem.shape == (batch, MEM_LEN, D_MODEL)
    n = batch * seq
    xf = x.reshape(n, D_MODEL)
    xb = xf.astype(BF16)
    mem_bf = mem.reshape(batch * MEM_LEN, D_MODEL).astype(BF16)
    q_block = 3 * MIX_WIDTH // COL_BLOCK

    for i in range(DEPTH):
        kind, j = i % N_MIXERS, i // N_MIXERS
        if kind == 0:
            z, mix, _ = _rwkv_mixer(xb, rwkv_w_in[j], rwkv_mu[j], rwkv_w0[j], rwkv_w2[j], rwkv_a0[j], rwkv_a2[j],
                                    rwkv_g2[j], rwkv_k_k[j], rwkv_k_a[j], rwkv_r_k[j], rwkv_gn_g[j], rwkv_gn_b[j],
                                    batch, seq)
        elif kind == 1:
            z = _matmul(xb, conv_w_in[j].astype(BF16), BF16, "conv_in_proj")
            mix = _short_conv(z, conv_w[j], batch, seq)
        else:
            z = _matmul(xb, sb_w_in[j].astype(BF16), BF16, "sb_in_proj")
            mix = _stick_breaking(z, batch, seq)
        w_kv = jnp.concatenate([mem_wk[i], mem_wv[i]], axis=1).astype(BF16)
        kv = _matmul(mem_bf, w_kv, BF16, "mem_kv_proj")
        mem_out = _mem_attention(z, kv, batch, seq, q_block)
        w_o = w_out[i].astype(BF16)
        xf, x_tiles = _outproj_ln(mix, mem_out, w_o[:MIX_WIDTH], w_o[MIX_WIDTH:], xf,
                                  ln_mix_g[i][None, :], ln_mix_b[i][None, :])
        xf, xb = _moe_ln(xf, x_tiles, router_group_w[i], router_group_b[i], router_expert_w[i], router_expert_b[i],
                         expert_w_gate, expert_w_up, expert_w_down, i, ln_ffn_g[i], ln_ffn_b[i])
    return xf.reshape(batch, seq, D_MODEL)
```

```python
import functools
import math

import jax
import jax.numpy as jnp
from jax import lax
from jax.experimental import pallas as pl
from jax.experimental.pallas import tpu as pltpu

F32 = jnp.float32
BF16 = jnp.bfloat16

D_MODEL = 2048
DEPTH = 4
N_MIXERS = 3
MEM_LEN = 256
MIX_WIDTH = 1536
MEM_HEADS = 4
MEM_HEAD_DIM = 128
MEM_WIDTH = 512
RWKV_HEAD_DIM = 64
RWKV_HEADS = 24
DECAY_LORA = 96
AAA_LORA = 96
GATE_LORA = 256
RWKV_GN_EPS = 64e-5
RWKV_COLS = 3 * MIX_WIDTH + DECAY_LORA + AAA_LORA + GATE_LORA
SB_HEAD_DIM = 128
SB_HEADS = 12
N_GROUPS = 4
EXPERTS_PER_GROUP = 8
N_EXPERTS = 32
TOP_K = 2
D_EXPERT = 512
DEEPNORM_ALPHA = (2 * DEPTH) ** 0.25
LN_EPS = 1e-5

LANES = 128
V7X_VMEM_LIMIT_BYTES = 56 << 20

COL_BLOCK = 512
MOE_BLOCK = 256
PLAN_BLOCK = 512
GATHER_UNROLL = 8
TOKEN_TILE_ROWS = D_MODEL // LANES
WKV_CHUNK = 64
WKV_PAIR = 2 * RWKV_HEAD_DIM
WKV_CHUNKS_PER_STEP = 8
SB_Q_BLOCK = 256
SB_GROUP = 4


def _params(*sem):
    return pltpu.CompilerParams(dimension_semantics=sem, vmem_limit_bytes=V7X_VMEM_LIMIT_BYTES)


def _dot(a, b):
    return jnp.dot(a, b, preferred_element_type=F32)


def _dot_nt(a, b):
    return lax.dot_general(a, b, (((1,), (1,)), ((), ())), preferred_element_type=F32)


def _dot_tn(a, b):
    return lax.dot_general(a, b, (((0,), (0,)), ((), ())), preferred_element_type=F32)


def _split2(x):
    hi = x.astype(BF16)
    lo = (x - hi.astype(F32)).astype(BF16)
    return hi, lo


def _dot_hl(x, w):
    hi, lo = _split2(x)
    return _dot(hi, w) + _dot(lo, w)


def _pick_tile(n, target, quantum=LANES):
    best = None
    for t in range(quantum, min(n, target) + 1, quantum):
        if n % t == 0:
            best = t
    assert best is not None, (n, target)
    return best


def _layer_norm_rows(y, g, b):
    mu = jnp.mean(y, axis=-1, keepdims=True)
    yc = y - mu
    var = jnp.mean(yc * yc, axis=-1, keepdims=True)
    return yc * lax.rsqrt(var + LN_EPS) * g + b


def _matmul_kernel(x_ref, w_ref, o_ref):
    o_ref[...] = _dot(x_ref[...], w_ref[...]).astype(o_ref.dtype)


def _matmul(x, w, out_dtype, name):
    m, k = x.shape
    n = w.shape[1]
    tm = _pick_tile(m, 1024, 8)
    tn = _pick_tile(n, 1536)
    return pl.pallas_call(
        _matmul_kernel,
        grid=(n // tn, m // tm),
        in_specs=[pl.BlockSpec((tm, k), lambda j, i: (i, 0)),
                  pl.BlockSpec((k, tn), lambda j, i: (0, j))],
        out_specs=pl.BlockSpec((tm, tn), lambda j, i: (i, j)),
        out_shape=jax.ShapeDtypeStruct((m, n), out_dtype),
        compiler_params=_params("parallel", "arbitrary"),
        name=name,
    )(x, w)


def _mem_attn_kernel(q_ref, kv_ref, o_ref):
    scale = 1.0 / math.sqrt(MEM_HEAD_DIM)
    for h in range(MEM_HEADS):
        sl = slice(h * MEM_HEAD_DIM, (h + 1) * MEM_HEAD_DIM)
        q = q_ref[:, sl]
        km = kv_ref[:, sl]
        vm = kv_ref[:, MEM_WIDTH + h * MEM_HEAD_DIM:MEM_WIDTH + (h + 1) * MEM_HEAD_DIM]
        s = _dot_nt(q, km) * scale
        s = s - jnp.max(s, axis=-1, keepdims=True)
        p = jnp.exp(s)
        p = p / jnp.sum(p, axis=-1, keepdims=True)
        o_ref[:, sl] = _dot(p.astype(BF16), vm).astype(o_ref.dtype)


def _mem_attention(z, kv, batch, seq, q_col_block):
    tq = _pick_tile(seq, 512, 8)
    nt = seq // tq
    return pl.pallas_call(
        _mem_attn_kernel,
        grid=(batch, nt),
        in_specs=[pl.BlockSpec((tq, MEM_WIDTH), lambda b, i: (b * nt + i, q_col_block)),
                  pl.BlockSpec((MEM_LEN, 2 * MEM_WIDTH), lambda b, i: (b, 0))],
        out_specs=pl.BlockSpec((tq, MEM_WIDTH), lambda b, i: (b * nt + i, 0)),
        out_shape=jax.ShapeDtypeStruct((batch * seq, MEM_WIDTH), BF16),
        compiler_params=_params("parallel", "parallel"),
        name="mem_attention",
    )(z, kv)


def _outproj_ln_kernel(mix_ref, mem_ref, w1_ref, w2_ref, x_ref, g_ref, b_ref, xo_ref, xt_ref):
    h = _dot(mix_ref[...], w1_ref[...]) + _dot(mem_ref[...], w2_ref[...])
    y = DEEPNORM_ALPHA * x_ref[...] + h
    out = _layer_norm_rows(y, g_ref[...], b_ref[...])
    xo_ref[...] = out
    _to_token_tiles(xt_ref, out)


def _outproj_ln(mix, mem_out, w_mix, w_mem, x, g, b):
    n = x.shape[0]
    tm = _pick_tile(n, 512, 8)
    row = lambda i: (i, 0)
    fixed = lambda i: (0, 0)
    return pl.pallas_call(
        _outproj_ln_kernel,
        grid=(n // tm,),
        in_specs=[pl.BlockSpec((tm, MIX_WIDTH), row),
                  pl.BlockSpec((tm, MEM_WIDTH), row),
                  pl.BlockSpec((MIX_WIDTH, D_MODEL), fixed),
                  pl.BlockSpec((MEM_WIDTH, D_MODEL), fixed),
                  pl.BlockSpec((tm, D_MODEL), row),
                  pl.BlockSpec((1, D_MODEL), fixed),
                  pl.BlockSpec((1, D_MODEL), fixed)],
        out_specs=[pl.BlockSpec((tm, D_MODEL), row), pl.BlockSpec((tm * TOKEN_TILE_ROWS, LANES), row)],
        out_shape=[jax.ShapeDtypeStruct((n, D_MODEL), F32),
                   jax.ShapeDtypeStruct((n * TOKEN_TILE_ROWS, LANES), F32)],
        compiler_params=_params("parallel"),
        name="outproj_ln",
    )(mix, mem_out, w_mix, w_mem, x, g, b)


def _conv_kernel(b_ref, c_ref, h_ref, w_ref, o_ref, carry_ref):
    @pl.when(pl.program_id(1) == 0)
    def _():
        carry_ref[...] = jnp.zeros_like(carry_ref)

    u = c_ref[...].astype(F32) * h_ref[...].astype(F32)
    tt = u.shape[0]
    rowid = lax.broadcasted_iota(jnp.int32, u.shape, 0)
    c0 = carry_ref[0:1, :]
    c1 = carry_ref[1:2, :]
    u1 = jnp.where(rowid == 0, c1, pltpu.roll(u, 1, axis=0))
    u2 = jnp.where(rowid == 0, c0, jnp.where(rowid == 1, c1, pltpu.roll(u, 2, axis=0)))
    carry_ref[0:2, :] = u[tt - 2:tt, :]
    y = w_ref[0:1, :] * u2 + w_ref[1:2, :] * u1 + w_ref[2:3, :] * u
    o_ref[...] = (b_ref[...].astype(F32) * y).astype(o_ref.dtype)


def _short_conv(z, conv_w, batch, seq):
    tt = _pick_tile(seq, 512, 8)
    nt = seq // tt
    spec = lambda c: pl.BlockSpec((tt, MIX_WIDTH), lambda b, i: (b * nt + i, c))
    return pl.pallas_call(
        _conv_kernel,
        grid=(batch, nt),
        in_specs=[spec(0), spec(1), spec(2),
                  pl.BlockSpec((3, MIX_WIDTH), lambda b, i: (0, 0))],
        out_specs=pl.BlockSpec((tt, MIX_WIDTH), lambda b, i: (b * nt + i, 0)),
        out_shape=jax.ShapeDtypeStruct((batch * seq, MIX_WIDTH), BF16),
        scratch_shapes=[pltpu.VMEM((8, MIX_WIDTH), F32)],
        compiler_params=_params("parallel", "arbitrary"),
        name="short_conv",
    )(z, z, z, conv_w)


def _sb_kernel(q_ref, k_ref, v_ref, o_ref):
    i = pl.program_id(2)
    tq = q_ref.shape[0]
    bk = tq
    scale = 1.0 / math.sqrt(SB_HEAD_DIM)
    q = q_ref[...]
    r = lax.broadcasted_iota(jnp.int32, (2 * bk, bk), 0)
    c = lax.broadcasted_iota(jnp.int32, (2 * bk, bk), 1)
    suffix = jnp.where((r & (bk - 1)) > c, 1.0, 0.0).astype(BF16)

    def rows_of(ref, j):
        return ref[pl.ds(pl.multiple_of(j * bk, bk), bk), :]

    def scores(j):
        return _dot_nt(q, rows_of(k_ref, j)) * scale

    def logs(s, valid):
        soft = jnp.log(1.0 + jnp.exp(-jnp.abs(s)))
        log_beta = jnp.minimum(s, 0.0) - soft
        log_not = log_beta - s
        if valid is not None:
            log_not = jnp.where(valid, log_not, 0.0)
        hi, lo = _split2(log_not)
        return log_beta, jnp.concatenate([hi, lo], axis=1), jnp.sum(log_not, axis=-1, keepdims=True)

    def weights(log_beta, within, later, valid):
        attn = jnp.exp(log_beta + within + later)
        if valid is not None:
            attn = jnp.where(valid, attn, 0.0)
        return attn.astype(BF16)

    def blocks(js, later, valid):
        s = [scores(j) for j in js]
        lg = [logs(x, valid) for x in s]
        within = [_dot(x[1], suffix) for x in lg]
        pv = None
        for j, (log_beta, _, total), w in zip(js, lg, within):
            p = _dot(weights(log_beta, w, later, valid), rows_of(v_ref, j))
            pv = p if pv is None else pv + p
            later = later + total
        return pv, later

    valid = lax.broadcasted_iota(jnp.int32, (tq, bk), 1) < lax.broadcasted_iota(jnp.int32, (tq, bk), 0)
    acc, later = blocks([i], jnp.zeros((tq, 1), F32), valid)

    def group(width, first):
        def body(step, carry):
            acc, later = carry
            j = first - width * step
            pv, later = blocks([j - d for d in range(width)], later, None)
            return acc + pv, later
        return body

    n1 = i & 1
    n2 = lax.shift_right_logical(i, 1) & 1
    carry = lax.fori_loop(0, n1, group(1, i - 1), (acc, later))
    carry = lax.fori_loop(0, n2, group(2, i - 1 - n1), carry)
    acc, _ = lax.fori_loop(0, lax.shift_right_logical(i, 2), group(SB_GROUP, i - 1 - n1 - 2 * n2), carry)
    o_ref[...] = acc.astype(o_ref.dtype)


def _stick_breaking(z, batch, seq):
    tq = _pick_tile(seq, SB_Q_BLOCK, LANES)
    nq = seq // tq
    return pl.pallas_call(
        _sb_kernel,
        grid=(batch, SB_HEADS, nq),
        in_specs=[pl.BlockSpec((tq, SB_HEAD_DIM), lambda b, h, i: (b * nq + i, h)),
                  pl.BlockSpec((seq, SB_HEAD_DIM), lambda b, h, i: (b, SB_HEADS + h)),
                  pl.BlockSpec((seq, SB_HEAD_DIM), lambda b, h, i: (b, 2 * SB_HEADS + h))],
        out_specs=pl.BlockSpec((tq, SB_HEAD_DIM), lambda b, h, i: (b * nq + i, h)),
        out_shape=jax.ShapeDtypeStruct((batch * seq, MIX_WIDTH), BF16),
        compiler_params=_params("parallel", "parallel", "arbitrary"),
        name="stick_breaking",
    )(z, z, z)


def _rwkv_prep_kernel(zr_ref, zk_ref, zv_ref, zl_ref, mur_ref, muk_ref, muv_ref, mul_ref,
                      w2_ref, a2_ref, g2_ref, w0_ref, a0_ref, kk_ref, ka_ref, rk_ref,
                      seg_ref, segt_ref,
                      r_out, k_out, v_out, an_out, bb_out, lw_out, gate_out, bonus_out,
                      carry_ref):
    @pl.when(pl.program_id(2) == 0)
    def _():
        carry_ref[...] = jnp.zeros_like(carry_ref)

    def shifted(z_ref, slot, mu_ref):
        z = z_ref[...].astype(F32)
        tt = z.shape[0]
        rowid = lax.broadcasted_iota(jnp.int32, z.shape, 0)
        prev = jnp.where(rowid == 0, carry_ref[slot:slot + 1, :], pltpu.roll(z, 1, axis=0))
        carry_ref[slot:slot + 1, :] = z[tt - 1:tt, :]
        return z + (prev - z) * mu_ref[...]

    r = shifted(zr_ref, 0, mur_ref)
    k = shifted(zk_ref, 1, muk_ref)
    v = shifted(zv_ref, 2, muv_ref)
    zl = shifted(zl_ref, 3, mul_ref)
    zw = zl[:, 0:LANES]
    za = zl[:, LANES:2 * LANES]
    zg = zl[:, 2 * LANES:4 * LANES]

    dw = w0_ref[...] + _dot(jnp.tanh(zw).astype(BF16), w2_ref[...])
    neg = -dw
    softplus = jnp.maximum(neg, 0.0) + jnp.log(1.0 + jnp.exp(-jnp.abs(neg)))
    lw = -jnp.exp(-softplus - 0.5)
    a = jax.nn.sigmoid(a0_ref[...] + _dot(za.astype(BF16), a2_ref[...]))
    gate = _dot(jax.nn.sigmoid(zg).astype(BF16), g2_ref[...])

    seg = seg_ref[...]
    segt = segt_ref[...]
    kk = k * kk_ref[...]
    norm = jnp.sqrt(_dot_hl(_dot_hl(kk * kk, seg), segt))
    kkn = kk / jnp.maximum(norm, 1e-12)
    k2 = k * (1.0 + (a - 1.0) * ka_ref[...])
    bonus = _dot_hl(_dot_hl(r * k2 * rk_ref[...], seg), segt) * v

    r_out[...] = r.astype(r_out.dtype)
    k_out[...] = k2.astype(k_out.dtype)
    v_out[...] = v.astype(v_out.dtype)
    an_out[...] = (-kkn).astype(an_out.dtype)
    bb_out[...] = (kkn * a).astype(bb_out.dtype)
    lw_out[...] = lw
    gate_out[...] = gate.astype(gate_out.dtype)
    bonus_out[...] = bonus.astype(bonus_out.dtype)


def _rwkv_prep(z, mu_main, mu_lora, w2p, a2p, g2, w0, a0, k_k, k_a, r_k, batch, seq):
    n = batch * seq
    tt = _pick_tile(seq, 256, 8)
    nt = seq // tt
    nc = MIX_WIDTH // COL_BLOCK
    heads_per_block = COL_BLOCK // RWKV_HEAD_DIM
    ch = jnp.arange(COL_BLOCK)[:, None] // RWKV_HEAD_DIM
    seg = (ch == jnp.arange(LANES)[None, :]).astype(BF16)
    segt = seg.T
    assert heads_per_block <= LANES

    zspec = lambda off: pl.BlockSpec((tt, COL_BLOCK), lambda c, b, i: (b * nt + i, off + c))
    zlspec = pl.BlockSpec((tt, COL_BLOCK), lambda c, b, i: (b * nt + i, 3 * nc + 1))
    muspec = lambda off: pl.BlockSpec((1, COL_BLOCK), lambda c, b, i: (0, off + c))
    colspec = lambda rows: pl.BlockSpec((rows, COL_BLOCK), lambda c, b, i: (0, c))
    fixed = lambda shape: pl.BlockSpec(shape, lambda c, b, i: (0, 0))
    ospec = pl.BlockSpec((tt, COL_BLOCK), lambda c, b, i: (b * nt + i, c))
    outs = pl.pallas_call(
        _rwkv_prep_kernel,
        grid=(nc, batch, nt),
        in_specs=[zspec(0), zspec(nc), zspec(2 * nc), zlspec,
                  muspec(0), muspec(nc), muspec(2 * nc), fixed((1, COL_BLOCK)),
                  colspec(LANES), colspec(LANES), colspec(GATE_LORA),
                  colspec(1), colspec(1), colspec(1), colspec(1), colspec(1),
                  fixed((COL_BLOCK, LANES)), fixed((LANES, COL_BLOCK))],
        out_specs=[ospec] * 8,
        out_shape=[jax.ShapeDtypeStruct((n, MIX_WIDTH), F32 if name == "lw" else BF16)
                   for name in ("r", "k", "v", "an", "bb", "lw", "gate", "bonus")],
        scratch_shapes=[pltpu.VMEM((8, COL_BLOCK), F32)],
        compiler_params=_params("parallel", "parallel", "arbitrary"),
        name="rwkv_prep",
    )(z, z, z, z, mu_main, mu_main, mu_main, mu_lora, w2p, a2p, g2, w0, a0, k_k, k_a, r_k, seg, segt)
    return outs


def _each(fn, *cols):
    return [fn(*args) for args in zip(*cols)]


def _wkv_chunk_terms(chunks, consts):
    tri, lane_lo, strict, incl, blk16, eye = consts
    c = WKV_CHUNK
    n2 = 2 * c

    def pair_rows(x):
        return jnp.concatenate([jnp.where(lane_lo, x, 0.0), jnp.where(lane_lo, 0.0, x)], axis=0)

    def mm(x, y):
        return _dot(x.astype(BF16), y.astype(BF16))

    def cumulative(r, k, v, an, bb, lw):
        h1 = lw.astype(BF16)
        r1 = lw - h1.astype(F32)
        h2 = r1.astype(BF16)
        h3 = (r1 - h2.astype(F32)).astype(BF16)
        return _dot(tri, h1) + _dot(tri, h2) + _dot(tri, h3)

    cums = _each(cumulative, *zip(*chunks))

    def operands(chunk, cum):
        r, k, v, an, bb, lw = chunk
        cum_last = cum[c - 1:c, :]
        p_inv = jnp.exp(-cum)
        p_tail = jnp.exp(cum_last - cum)
        a_p = pair_rows(an * jnp.exp(cum - lw)).astype(BF16)
        q_p = pair_rows(r * jnp.exp(cum))
        b_p = pair_rows(bb * p_inv).astype(BF16)
        k_p = pair_rows(k * p_inv).astype(BF16)
        v_p = pair_rows(v).astype(BF16)
        bt_p = pair_rows(bb * p_tail).astype(BF16)
        kt_p = pair_rows(k * p_tail).astype(BF16)
        aq = jnp.concatenate([a_p, q_p.astype(BF16)], axis=0)
        bk = jnp.concatenate([b_p, k_p], axis=0)
        return a_p, q_p, v_p, jnp.concatenate([bt_p, kt_p], axis=0), aq, bk, jnp.exp(cum_last)

    a_p, q_p, v_p, btkt, aq, bk, pc = zip(*_each(operands, chunks, cums))
    g = _each(_dot_nt, aq, bk)
    l_ab = [jnp.where(strict, x[:n2, :n2], 0.0) for x in g]
    a_rb = [jnp.where(incl, x[n2:, :n2], 0.0).astype(BF16) for x in g]
    a_kk = [jnp.concatenate([jnp.where(strict, x[:n2, n2:], 0.0), jnp.where(incl, x[n2:, n2:], 0.0)],
                            axis=0).astype(BF16) for x in g]
    av = _each(_dot, a_kk, v_p)

    ld = [jnp.where(blk16, x, 0.0) for x in l_ab]
    lo = [x - y for x, y in zip(l_ab, ld)]
    l2 = _each(mm, ld, ld)
    l4 = _each(mm, l2, l2)
    l8 = _each(mm, l4, l4)
    dinv = _each(mm, [eye + x for x in ld], [eye + x for x in l2])
    dinv = _each(mm, dinv, [eye + x for x in l4])
    dinv = _each(mm, dinv, [eye + x for x in l8])
    m1 = _each(mm, dinv, lo)
    m2 = _each(mm, m1, m1)
    tinv = _each(mm, [eye + x for x in m1], [eye + x for x in m2])
    tinv = _each(mm, tinv, dinv)

    wu = _each(mm, tinv, [jnp.concatenate([a, x[:n2].astype(BF16)], axis=1) for a, x in zip(a_p, av)])
    wu_b = [x.astype(BF16) for x in wu]
    rb = _each(_dot, a_rb, wu_b)
    qw = [q + x[:, :n2] for q, x in zip(q_p, rb)]
    y0 = [x[:, n2:] + y[n2:] for x, y in zip(rb, av)]
    e = _each(_dot_tn, [x[:, :n2] for x in wu_b], [x[:n2] for x in btkt])
    f = _each(_dot_tn, [jnp.concatenate([x[:, n2:], y], axis=0) for x, y in zip(wu_b, v_p)], btkt)
    return [(a[:c] + a[c:], b[:c] + b[c:], ee, ff, p) for a, b, ee, ff, p in zip(qw, y0, e, f, pc)]


def _wkv_terms_kernel(r_ref, k_ref, v_ref, an_ref, bb_ref, lw_ref, qw_ref, y0_ref, e_ref, f_ref, pc_ref):
    c = WKV_CHUNK
    n2 = 2 * c
    row = lax.broadcasted_iota(jnp.int32, (n2, n2), 0)
    col = lax.broadcasted_iota(jnp.int32, (n2, n2), 1)
    tri_r = lax.broadcasted_iota(jnp.int32, (c, c), 0)
    tri_c = lax.broadcasted_iota(jnp.int32, (c, c), 1)
    consts = (
        jnp.where(tri_c <= tri_r, 1.0, 0.0).astype(BF16),
        lax.broadcasted_iota(jnp.int32, (c, WKV_PAIR), 1) < RWKV_HEAD_DIM,
        row > col,
        row >= col,
        (row >> 4) == (col >> 4),
        jnp.where(row == col, 1.0, 0.0),
    )
    n_chunks = r_ref.shape[0] // c
    chunks = []
    for j in range(n_chunks):
        rows = slice(j * c, (j + 1) * c)
        chunks.append(tuple(ref[rows, :].astype(F32) for ref in (r_ref, k_ref, v_ref, an_ref, bb_ref, lw_ref)))
    for j, (qw, y0, e, f, pc) in enumerate(_wkv_chunk_terms(chunks, consts)):
        rows = slice(j * c, (j + 1) * c)
        qw_ref[rows, :] = qw.astype(qw_ref.dtype)
        y0_ref[rows, :] = y0
        e_ref[j * n2:(j + 1) * n2, :] = e.astype(e_ref.dtype)
        f_ref[j * n2:(j + 1) * n2, :] = f
        pc_ref[j * 8:(j + 1) * 8, :] = jnp.broadcast_to(pc, (8, WKV_PAIR))


def _wkv_terms(r, k, v, an, bb, lw, batch, seq):
    n = batch * seq
    c = WKV_CHUNK
    n_pairs = MIX_WIDTH // WKV_PAIR
    cps = _pick_tile(seq // c, WKV_CHUNKS_PER_STEP, 1)
    steps = seq // (c * cps)
    tok = pl.BlockSpec((cps * c, WKV_PAIR), lambda b, p, i: (b * steps + i, p))
    mat = pl.BlockSpec((cps * WKV_PAIR, WKV_PAIR), lambda b, p, i: (b * steps + i, p))
    vec = pl.BlockSpec((cps * 8, WKV_PAIR), lambda b, p, i: (b * steps + i, p))
    n_chunks = n // c
    return pl.pallas_call(
        _wkv_terms_kernel,
        grid=(batch, n_pairs, steps),
        in_specs=[tok] * 6,
        out_specs=[tok, tok, mat, mat, vec],
        out_shape=[jax.ShapeDtypeStruct((n, MIX_WIDTH), BF16),
                   jax.ShapeDtypeStruct((n, MIX_WIDTH), F32),
                   jax.ShapeDtypeStruct((n_chunks * WKV_PAIR, MIX_WIDTH), BF16),
                   jax.ShapeDtypeStruct((n_chunks * WKV_PAIR, MIX_WIDTH), F32),
                   jax.ShapeDtypeStruct((n_chunks * 8, MIX_WIDTH), F32)],
        compiler_params=_params("parallel", "parallel", "parallel"),
        name="wkv_terms",
    )(r, k, v, an, bb, lw)


def _wkv_scan_kernel(qw_ref, y0_ref, e_ref, f_ref, pc_ref, bonus_ref, gate_ref, gng_ref, gnb_ref,
                     o_ref, state_ref):
    @pl.when(pl.program_id(2) == 0)
    def _():
        state_ref[...] = jnp.zeros_like(state_ref)

    n_pairs = state_ref.shape[1] // WKV_PAIR
    lane_lo = lax.broadcasted_iota(jnp.int32, (WKV_CHUNK, WKV_PAIR), 1) < RWKV_HEAD_DIM
    inv_n = 1.0 / RWKV_HEAD_DIM
    for p in range(n_pairs):
        sl = slice(p * WKV_PAIR, (p + 1) * WKV_PAIR)
        s = state_ref[:, sl]
        sb = s.astype(BF16)
        y = _dot_nt(qw_ref[:, sl], sb) + y0_ref[:, sl]
        state_ref[:, sl] = s * pc_ref[0:1, sl] + _dot(sb, e_ref[:, sl]) + f_ref[:, sl]

        def head_mean(x):
            lo = jnp.sum(jnp.where(lane_lo, x, 0.0), axis=-1, keepdims=True)
            hi = jnp.sum(jnp.where(lane_lo, 0.0, x), axis=-1, keepdims=True)
            return jnp.where(lane_lo, lo, hi) * inv_n

        yc = y - head_mean(y)
        var = head_mean(yc * yc)
        yn = yc * lax.rsqrt(var + RWKV_GN_EPS) * gng_ref[:, sl] + gnb_ref[:, sl]
        o_ref[:, sl] = ((yn + bonus_ref[:, sl].astype(F32)) * gate_ref[:, sl].astype(F32)).astype(o_ref.dtype)


def _wkv_scan(qw, y0, e, f, pc, bonus, gate, gn_g, gn_b, batch, seq):
    n = batch * seq
    c = WKV_CHUNK
    nc = seq // c
    width = MIX_WIDTH // 2
    groups = MIX_WIDTH // width
    tok = pl.BlockSpec((c, width), lambda b, g, i: (b * nc + i, g))
    mat = pl.BlockSpec((WKV_PAIR, width), lambda b, g, i: (b * nc + i, g))
    vec = pl.BlockSpec((8, width), lambda b, g, i: (b * nc + i, g))
    par = pl.BlockSpec((1, width), lambda b, g, i: (0, g))
    return pl.pallas_call(
        _wkv_scan_kernel,
        grid=(batch, groups, nc),
        in_specs=[tok, tok, mat, mat, vec, tok, tok, par, par],
        out_specs=tok,
        out_shape=jax.ShapeDtypeStruct((n, MIX_WIDTH), BF16),
        scratch_shapes=[pltpu.VMEM((WKV_PAIR, width), F32)],
        compiler_params=_params("parallel", "parallel", "arbitrary"),
        name="wkv_scan",
    )(qw, y0, e, f, pc, bonus, gate, gn_g, gn_b)


def _router_kernel(x_ref, w_ref, b_ref, o_ref):
    logits = jnp.dot(x_ref[...], w_ref[...], preferred_element_type=F32,
                     precision=lax.Precision.HIGHEST) + b_ref[...]
    lane = lax.broadcasted_iota(jnp.int32, logits.shape, 1)
    neg_inf = -jnp.inf
    big = 4 * LANES

    def first_where(mask):
        return jnp.min(jnp.where(mask, lane, big), axis=-1, keepdims=True)

    gl = jnp.where(lane < N_GROUPS, logits, neg_inf)
    gmax = jnp.max(gl, axis=-1, keepdims=True)
    gidx = first_where(gl == gmax)
    p_group_sel = 1.0 / jnp.sum(jnp.exp(gl - gmax), axis=-1, keepdims=True)

    lo = N_GROUPS + EXPERTS_PER_GROUP * gidx
    emask = (lane >= lo) & (lane < lo + EXPERTS_PER_GROUP)
    el = jnp.where(emask, logits, neg_inf)
    emax = jnp.max(el, axis=-1, keepdims=True)
    pe = jnp.exp(el - emax)
    pe = pe / jnp.sum(pe, axis=-1, keepdims=True)
    pe = jnp.where(emask, pe, -1.0)
    p1 = jnp.max(pe, axis=-1, keepdims=True)
    i1 = first_where(pe == p1)
    pe2 = jnp.where(lane == i1, -1.0, pe)
    p2 = jnp.max(pe2, axis=-1, keepdims=True)
    i2 = first_where(pe2 == p2)
    denom = p1 + p2
    g1 = p_group_sel * p1 / denom
    g2 = p_group_sel * p2 / denom
    e1 = (i1 - N_GROUPS).astype(F32)
    e2 = (i2 - N_GROUPS).astype(F32)
    o_ref[...] = jnp.where(lane == 0, e1, jnp.where(lane == 1, e2, jnp.where(lane == 2, g1, jnp.where(lane == 3, g2, 0.0))))


def _router(x, w_router, b_router):
    n = x.shape[0]
    tm = _pick_tile(n, 512, 8)
    return pl.pallas_call(
        _router_kernel,
        grid=(n // tm,),
        in_specs=[pl.BlockSpec((tm, D_MODEL), lambda i: (i, 0)),
                  pl.BlockSpec((D_MODEL, LANES), lambda i: (0, 0)),
                  pl.BlockSpec((1, LANES), lambda i: (0, 0))],
        out_specs=pl.BlockSpec((tm, LANES), lambda i: (i, 0)),
        out_shape=jax.ShapeDtypeStruct((n, LANES), F32),
        compiler_params=_params("parallel"),
        name="moe_router",
    )(x, w_router, b_router)


def _plan_kernel(e_ref, rank_ref, count_ref, carry_ref):
    @pl.when(pl.program_id(0) == 0)
    def _():
        carry_ref[...] = jnp.zeros_like(carry_ref)

    nb = e_ref.shape[-1]
    e = e_ref[0]
    onehot = jnp.where(lax.broadcasted_iota(jnp.int32, (N_EXPERTS, nb), 0) == e, 1.0, 0.0)
    r = lax.broadcasted_iota(jnp.int32, (nb, nb), 0)
    c = lax.broadcasted_iota(jnp.int32, (nb, nb), 1)
    before = jnp.where(r < c, 1.0, 0.0).astype(BF16)
    seen = _dot(onehot.astype(BF16), before) + carry_ref[:, 0:1]
    rank_ref[0] = jnp.sum(onehot * seen, axis=0, keepdims=True).astype(jnp.int32)
    carry_ref[...] = carry_ref[...] + jnp.sum(onehot, axis=1, keepdims=True)
    count_ref[...] = carry_ref[...]


def _dispatch_plan(e_flat):
    n_slots = e_flat.shape[0]
    nb = _pick_tile(n_slots, PLAN_BLOCK)
    steps = n_slots // nb
    rank, counts = pl.pallas_call(
        _plan_kernel,
        grid=(steps,),
        in_specs=[pl.BlockSpec((1, 1, nb), lambda i: (i, 0, 0))],
        out_specs=[pl.BlockSpec((1, 1, nb), lambda i: (i, 0, 0)),
                   pl.BlockSpec((N_EXPERTS, LANES), lambda i: (0, 0))],
        out_shape=[jax.ShapeDtypeStruct((steps, 1, nb), jnp.int32),
                   jax.ShapeDtypeStruct((N_EXPERTS, LANES), F32)],
        scratch_shapes=[pltpu.VMEM((N_EXPERTS, LANES), F32)],
        compiler_params=_params("arbitrary"),
        name="moe_plan",
    )(e_flat.reshape(steps, 1, nb))
    return rank.reshape(n_slots), counts[:, 0].astype(jnp.int32)


def _to_token_tiles(dst_ref, value):
    rows = value.shape[0]
    for j in range(TOKEN_TILE_ROWS):
        dst_ref[pl.ds(j, rows, stride=TOKEN_TILE_ROWS), :] = value[:, j * LANES:(j + 1) * LANES]


def _token_tile_columns(src_ref, rows, j):
    return src_ref[pl.ds(j, rows, stride=TOKEN_TILE_ROWS), :]


def _row_copy(table_hbm, dst_ref, src_row, dst_row, sem):
    src = pl.multiple_of(src_row * TOKEN_TILE_ROWS, TOKEN_TILE_ROWS)
    dst = pl.multiple_of(dst_row * TOKEN_TILE_ROWS, TOKEN_TILE_ROWS)
    return pltpu.make_async_copy(table_hbm.at[pl.ds(src, TOKEN_TILE_ROWS)],
                                 dst_ref.at[pl.ds(dst, TOKEN_TILE_ROWS)], sem)


def _token_gather(wait, idx_ref, offset, stride, n_tokens, table_hbm, dst_ref, sem):
    def group(g, carry):
        for u in range(GATHER_UNROLL):
            i = g * GATHER_UNROLL + u
            if wait:
                _row_copy(table_hbm, dst_ref, 0, i, sem).wait()
            else:
                _row_copy(table_hbm, dst_ref, idx_ref[0, 0, offset + stride * i], i, sem).start(priority=u % 2)
        return carry

    assert GATHER_UNROLL & (GATHER_UNROLL - 1) == 0
    groups = lax.shift_right_logical(jnp.asarray(n_tokens, jnp.int32) + (GATHER_UNROLL - 1),
                                     GATHER_UNROLL.bit_length() - 1)
    lax.fori_loop(0, groups, group, 0)


def _expert_kernel(be_ref, nu_ref, nv_ref, idx_ref, idx_next_ref, x_hbm, wg_ref, wu_ref, wd_ref, o_ref,
                   wg_s, wu_s, wd_s, xb_s, xbuf, sem):
    i = pl.program_id(0)
    n_used = nu_ref[0]
    slot = i & 1

    def gather(wait, refs, blk, slot):
        _token_gather(wait, refs, 0, 1, nv_ref[blk], x_hbm, xbuf.at[slot], sem.at[slot])

    @pl.when(i == 0)
    def _():
        xbuf[...] = jnp.zeros_like(xbuf)
        gather(False, idx_ref, 0, 0)

    @pl.when(i + 1 < n_used)
    def _():
        gather(False, idx_next_ref, i + 1, 1 - slot)

    prev = be_ref[jnp.maximum(i - 1, 0)]
    changed = (i == 0) | (be_ref[i] != prev)

    @pl.when(changed)
    def _():
        wg_s[...] = wg_ref[0, 0].astype(BF16)
        wu_s[...] = wu_ref[0, 0].astype(BF16)
        wd_s[...] = wd_ref[0, 0].astype(BF16)

    used = i < n_used

    @pl.when(used)
    def _():
        gather(True, idx_ref, i, slot)
        for j in range(TOKEN_TILE_ROWS):
            xb_s[:, j * LANES:(j + 1) * LANES] = _token_tile_columns(xbuf.at[slot], MOE_BLOCK, j).astype(BF16)
        xb = xb_s[...]
        g = _dot(xb, wg_s[...])
        u = _dot(xb, wu_s[...])
        h = (g * jax.nn.sigmoid(g)) * u
        _to_token_tiles(o_ref, _dot(h.astype(BF16), wd_s[...]))

    @pl.when(jnp.logical_not(used))
    def _():
        o_ref[...] = jnp.zeros_like(o_ref)


def _experts(x_tiles, src_tok, block_expert, n_used, n_valid, w_gate, w_up, w_down, layer):
    cap = src_tok.shape[0]
    n_blk = cap // MOE_BLOCK
    wspec = lambda shape: pl.BlockSpec((1, 1) + shape, lambda i, be, nu, nv: (layer, be[i], 0, 0))
    idx_spec = lambda f: pl.BlockSpec((1, 1, MOE_BLOCK), lambda i, be, nu, nv: (f(i), 0, 0),
                                      memory_space=pltpu.SMEM)
    grid_spec = pltpu.PrefetchScalarGridSpec(
        num_scalar_prefetch=3,
        grid=(n_blk,),
        in_specs=[idx_spec(lambda i: i), idx_spec(lambda i: jnp.minimum(i + 1, n_blk - 1)),
                  pl.BlockSpec(memory_space=pl.ANY),
                  wspec((D_MODEL, D_EXPERT)), wspec((D_MODEL, D_EXPERT)), wspec((D_EXPERT, D_MODEL))],
        out_specs=pl.BlockSpec((MOE_BLOCK * TOKEN_TILE_ROWS, LANES), lambda i, be, nu, nv: (i, 0)),
        scratch_shapes=[pltpu.VMEM((D_MODEL, D_EXPERT), BF16),
                        pltpu.VMEM((D_MODEL, D_EXPERT), BF16),
                        pltpu.VMEM((D_EXPERT, D_MODEL), BF16),
                        pltpu.VMEM((MOE_BLOCK, D_MODEL), BF16),
                        pltpu.VMEM((2, MOE_BLOCK * TOKEN_TILE_ROWS, LANES), F32),
                        pltpu.SemaphoreType.DMA((2,))],
    )
    idx = src_tok.reshape(n_blk, 1, MOE_BLOCK)
    return pl.pallas_call(
        _expert_kernel,
        grid_spec=grid_spec,
        out_shape=jax.ShapeDtypeStruct((cap * TOKEN_TILE_ROWS, LANES), F32),
        compiler_params=_params("arbitrary"),
        name="moe_experts",
    )(block_expert, n_used, n_valid, idx, idx, x_tiles, w_gate, w_up, w_down)


def _combine_ln_kernel(idx_ref, idx_next_ref, ys_hbm, slab_ref, x_ref, g_ref, b_ref, xo_ref, xb_ref,
                       buf, y_s, sem):
    i = pl.program_id(0)
    slot = i & 1
    tm = x_ref.shape[0]

    def gather(wait, refs, slot):
        for k in range(TOP_K):
            _token_gather(wait, refs, k, TOP_K, tm, ys_hbm, buf.at[slot, k], sem.at[slot, k])

    @pl.when(i == 0)
    def _():
        gather(False, idx_ref, 0)

    @pl.when(i + 1 < pl.num_programs(0))
    def _():
        gather(False, idx_next_ref, 1 - slot)

    slab = slab_ref[...]
    lane = lax.broadcasted_iota(jnp.int32, slab.shape, 1)
    g1 = jnp.sum(jnp.where(lane == 2, slab, 0.0), axis=-1, keepdims=True)
    g2 = jnp.sum(jnp.where(lane == 3, slab, 0.0), axis=-1, keepdims=True)
    gather(True, idx_ref, slot)
    for j in range(TOKEN_TILE_ROWS):
        cols = slice(j * LANES, (j + 1) * LANES)
        f = (_token_tile_columns(buf.at[slot, 0], tm, j) * g1
             + _token_tile_columns(buf.at[slot, 1], tm, j) * g2)
        y_s[:, cols] = DEEPNORM_ALPHA * x_ref[:, cols] + f
    out = _layer_norm_rows(y_s[...], g_ref[...], b_ref[...])
    xo_ref[...] = out
    xb_ref[...] = out.astype(BF16)


def _combine_ln(ys, dest, slab, x, g, b):
    n = x.shape[0]
    tm = _pick_tile(n, 256, 8)
    steps = n // tm
    row = lambda i: (i, 0)
    fixed = lambda i: (0, 0)
    idx_spec = lambda f: pl.BlockSpec((1, 1, TOP_K * tm), lambda i: (f(i), 0, 0), memory_space=pltpu.SMEM)
    idx = dest.reshape(steps, 1, TOP_K * tm)
    return pl.pallas_call(
        _combine_ln_kernel,
        grid=(steps,),
        in_specs=[idx_spec(lambda i: i), idx_spec(lambda i: jnp.minimum(i + 1, steps - 1)),
                  pl.BlockSpec(memory_space=pl.ANY),
                  pl.BlockSpec((tm, LANES), row),
                  pl.BlockSpec((tm, D_MODEL), row),
                  pl.BlockSpec((1, D_MODEL), fixed),
                  pl.BlockSpec((1, D_MODEL), fixed)],
        out_specs=[pl.BlockSpec((tm, D_MODEL), row), pl.BlockSpec((tm, D_MODEL), row)],
        out_shape=[jax.ShapeDtypeStruct((n, D_MODEL), F32), jax.ShapeDtypeStruct((n, D_MODEL), BF16)],
        scratch_shapes=[pltpu.VMEM((2, TOP_K, tm * TOKEN_TILE_ROWS, LANES), F32),
                        pltpu.VMEM((tm, D_MODEL), F32),
                        pltpu.SemaphoreType.DMA((2, TOP_K))],
        compiler_params=_params("arbitrary"),
        name="moe_combine_ln",
    )(idx, idx, ys, slab, x, g, b)


def _moe_ln(x, x_tiles, rg_w, rg_b, re_w, re_b, w_gate, w_up, w_down, layer, ln_g, ln_b):
    n = x.shape[0]
    n_slots = n * TOP_K
    pad_cols = LANES - N_GROUPS - N_EXPERTS
    w_router = jnp.concatenate([rg_w, re_w, jnp.zeros((D_MODEL, pad_cols), F32)], axis=1)
    b_router = jnp.concatenate([rg_b, re_b, jnp.zeros((pad_cols,), F32)])[None, :]
    slab = _router(x, w_router, b_router)

    e_flat = slab[:, :TOP_K].astype(jnp.int32).reshape(n_slots)
    rank, counts = _dispatch_plan(e_flat)
    padded = (counts + MOE_BLOCK - 1) // MOE_BLOCK * MOE_BLOCK
    pad_end = jnp.cumsum(padded)
    pad_start = pad_end - padded
    dest = pad_start[e_flat] + rank
    cap = n_slots + N_EXPERTS * MOE_BLOCK
    n_blk = cap // MOE_BLOCK
    src_tok = jnp.zeros((cap,), jnp.int32).at[dest].set(jnp.arange(n_slots, dtype=jnp.int32) // TOP_K)
    block_start = jnp.arange(n_blk, dtype=jnp.int32) * MOE_BLOCK
    block_expert = jnp.minimum(jnp.sum((pad_end[None, :] <= block_start[:, None]).astype(jnp.int32), axis=1),
                               N_EXPERTS - 1)
    n_used = (pad_end[-1:] // MOE_BLOCK).astype(jnp.int32)
    n_valid = jnp.clip(counts[block_expert] - (block_start - pad_start[block_expert]), 0, MOE_BLOCK)
    n_valid = jnp.where(block_start < pad_end[-1], n_valid, 0).astype(jnp.int32)

    ys = _experts(x_tiles, src_tok, block_expert, n_used, n_valid, w_gate, w_up, w_down, layer)
    return _combine_ln(ys, dest, slab, x, ln_g[None, :], ln_b[None, :])


def _rwkv_mixer(x_bf, w_in, mu, w0, w2, a0, a2, g2, k_k, k_a, r_k, gn_g, gn_b, batch, seq):
    c3 = 3 * MIX_WIDTH
    lo_w, lo_a, lo_g = c3, c3 + DECAY_LORA, c3 + DECAY_LORA + AAA_LORA
    pad = LANES - DECAY_LORA

    def permute_cols(t, with_mem):
        zeros = jnp.zeros(t.shape[:-1] + (pad,), t.dtype)
        parts = [t[..., :c3]]
        if with_mem:
            parts.append(t[..., RWKV_COLS:])
        parts += [t[..., lo_w:lo_a], zeros, t[..., lo_a:lo_g], zeros, t[..., lo_g:RWKV_COLS]]
        return jnp.concatenate(parts, axis=-1)

    w_perm = permute_cols(w_in, True).astype(BF16)
    mu_perm = permute_cols(mu[None, :], False)
    mu_main, mu_lora = mu_perm[:, :c3], mu_perm[:, c3:]
    z = _matmul(x_bf, w_perm, BF16, "rwkv_in_proj")

    zrow = jnp.zeros((pad, MIX_WIDTH), F32)
    w2p = jnp.concatenate([w2, zrow], axis=0).astype(BF16)
    a2p = jnp.concatenate([a2, zrow], axis=0).astype(BF16)
    r, k, v, an, bb, lw, gate, bonus = _rwkv_prep(
        z, mu_main, mu_lora, w2p, a2p, g2.astype(BF16), w0[None, :], a0[None, :], k_k[None, :], k_a[None, :],
        r_k.reshape(1, MIX_WIDTH), batch, seq)
    qw, y0, e, f, pc = _wkv_terms(r, k, v, an, bb, lw, batch, seq)
    mix = _wkv_scan(qw, y0, e, f, pc, bonus, gate, gn_g[None, :], gn_b[None, :], batch, seq)
    return z, mix, c3 // COL_BLOCK


def kernel(x, mem, w_out, mem_wk, mem_wv, ln_mix_g, ln_mix_b, ln_ffn_g, ln_ffn_b, router_group_w, router_group_b, router_expert_w, router_expert_b, expert_w_gate, expert_w_up, expert_w_down, rwkv_w_in, rwkv_mu, rwkv_w0, rwkv_w2, rwkv_a0, rwkv_a2, rwkv_g2, rwkv_k_k, rwkv_k_a, rwkv_r_k, rwkv_gn_g, rwkv_gn_b, conv_w_in, conv_w, sb_w_in):
    batch, seq, d = x.shape
    assert d == D_MODEL and mem.shape == (batch, MEM_LEN, D_MODEL)
    n = batch * seq
    xf = x.reshape(n, D_MODEL)
    xb = xf.astype(BF16)
    mem_bf = mem.reshape(batch * MEM_LEN, D_MODEL).astype(BF16)
    q_block = 3 * MIX_WIDTH // COL_BLOCK

    for i in range(DEPTH):
        kind, j = i % N_MIXERS, i // N_MIXERS
        if kind == 0:
            z, mix, _ = _rwkv_mixer(xb, rwkv_w_in[j], rwkv_mu[j], rwkv_w0[j], rwkv_w2[j], rwkv_a0[j], rwkv_a2[j],
                                    rwkv_g2[j], rwkv_k_k[j], rwkv_k_a[j], rwkv_r_k[j], rwkv_gn_g[j], rwkv_gn_b[j],
                                    batch, seq)
        elif kind == 1:
            z = _matmul(xb, conv_w_in[j].astype(BF16), BF16, "conv_in_proj")
            mix = _short_conv(z, conv_w[j], batch, seq)
        else:
            z = _matmul(xb, sb_w_in[j].astype(BF16), BF16, "sb_in_proj")
            mix = _stick_breaking(z, batch, seq)
        w_kv = jnp.concatenate([mem_wk[i], mem_wv[i]], axis=1).astype(BF16)
        kv = _matmul(mem_bf, w_kv, BF16, "mem_kv_proj")
        mem_out = _mem_attention(z, kv, batch, seq, q_block)
        w_o = w_out[i].astype(BF16)
        xf, x_tiles = _outproj_ln(mix, mem_out, w_o[:MIX_WIDTH], w_o[MIX_WIDTH:], xf,
                                  ln_mix_g[i][None, :], ln_mix_b[i][None, :])
        xf, xb = _moe_ln(xf, x_tiles, router_group_w[i], router_group_b[i], router_expert_w[i], router_expert_b[i],
                         expert_w_gate, expert_w_up, expert_w_down, i, ln_ffn_g[i], ln_ffn_b[i])
    return xf.reshape(batch, seq, D_MODEL)
```

```python
import functools
import math

import jax
import jax.numpy as jnp
from jax import lax
from jax.experimental import pallas as pl
from jax.experimental.pallas import tpu as pltpu

F32 = jnp.float32
BF16 = jnp.bfloat16

D_MODEL = 2048
DEPTH = 4
N_MIXERS = 3
MEM_LEN = 256
MIX_WIDTH = 1536
MEM_HEADS = 4
MEM_HEAD_DIM = 128
MEM_WIDTH = 512
RWKV_HEAD_DIM = 64
RWKV_HEADS = 24
DECAY_LORA = 96
AAA_LORA = 96
GATE_LORA = 256
RWKV_GN_EPS = 64e-5
RWKV_COLS = 3 * MIX_WIDTH + DECAY_LORA + AAA_LORA + GATE_LORA
SB_HEAD_DIM = 128
SB_HEADS = 12
N_GROUPS = 4
EXPERTS_PER_GROUP = 8
N_EXPERTS = 32
TOP_K = 2
D_EXPERT = 512
DEEPNORM_ALPHA = (2 * DEPTH) ** 0.25
LN_EPS = 1e-5

LANES = 128
V7X_VMEM_LIMIT_BYTES = 56 << 20

COL_BLOCK = 512
MOE_BLOCK = 256
MOE_ROW_SPLIT = 2
PLAN_BLOCK = 512
GATHER_UNROLL = 8
TOKEN_TILE_ROWS = D_MODEL // LANES
WKV_CHUNK = 64
WKV_PAIR = 2 * RWKV_HEAD_DIM
WKV_CHUNKS_PER_STEP = 8
SB_Q_BLOCK = 256
SB_GROUP = 4


def _params(*sem):
    return pltpu.CompilerParams(dimension_semantics=sem, vmem_limit_bytes=V7X_VMEM_LIMIT_BYTES)


def _dot(a, b):
    return jnp.dot(a, b, preferred_element_type=F32)


def _dot_nt(a, b):
    return lax.dot_general(a, b, (((1,), (1,)), ((), ())), preferred_element_type=F32)


def _dot_tn(a, b):
    return lax.dot_general(a, b, (((0,), (0,)), ((), ())), preferred_element_type=F32)


def _split2(x):
    hi = x.astype(BF16)
    lo = (x - hi.astype(F32)).astype(BF16)
    return hi, lo


def _dot_hl(x, w):
    hi, lo = _split2(x)
    return _dot(hi, w) + _dot(lo, w)


def _pick_tile(n, target, quantum=LANES):
    best = None
    for t in range(quantum, min(n, target) + 1, quantum):
        if n % t == 0:
            best = t
    assert best is not None, (n, target)
    return best


def _layer_norm_rows(y, g, b):
    mu = jnp.mean(y, axis=-1, keepdims=True)
    yc = y - mu
    var = jnp.mean(yc * yc, axis=-1, keepdims=True)
    return yc * lax.rsqrt(var + LN_EPS) * g + b


def _matmul_kernel(x_ref, w_ref, o_ref):
    o_ref[...] = _dot(x_ref[...], w_ref[...]).astype(o_ref.dtype)


def _matmul(x, w, out_dtype, name):
    m, k = x.shape
    n = w.shape[1]
    tm = _pick_tile(m, 1024, 8)
    tn = _pick_tile(n, 1536)
    return pl.pallas_call(
        _matmul_kernel,
        grid=(n // tn, m // tm),
        in_specs=[pl.BlockSpec((tm, k), lambda j, i: (i, 0)),
                  pl.BlockSpec((k, tn), lambda j, i: (0, j))],
        out_specs=pl.BlockSpec((tm, tn), lambda j, i: (i, j)),
        out_shape=jax.ShapeDtypeStruct((m, n), out_dtype),
        compiler_params=_params("parallel", "arbitrary"),
        name=name,
    )(x, w)


def _mem_attn_kernel(q_ref, kv_ref, o_ref):
    scale = 1.0 / math.sqrt(MEM_HEAD_DIM)
    for h in range(MEM_HEADS):
        sl = slice(h * MEM_HEAD_DIM, (h + 1) * MEM_HEAD_DIM)
        q = q_ref[:, sl]
        km = kv_ref[:, sl]
        vm = kv_ref[:, MEM_WIDTH + h * MEM_HEAD_DIM:MEM_WIDTH + (h + 1) * MEM_HEAD_DIM]
        s = _dot_nt(q, km) * scale
        s = s - jnp.max(s, axis=-1, keepdims=True)
        p = jnp.exp(s)
        p = p / jnp.sum(p, axis=-1, keepdims=True)
        o_ref[:, sl] = _dot(p.astype(BF16), vm).astype(o_ref.dtype)


def _mem_attention(z, kv, batch, seq, q_col_block):
    tq = _pick_tile(seq, 512, 8)
    nt = seq // tq
    return pl.pallas_call(
        _mem_attn_kernel,
        grid=(batch, nt),
        in_specs=[pl.BlockSpec((tq, MEM_WIDTH), lambda b, i: (b * nt + i, q_col_block)),
                  pl.BlockSpec((MEM_LEN, 2 * MEM_WIDTH), lambda b, i: (b, 0))],
        out_specs=pl.BlockSpec((tq, MEM_WIDTH), lambda b, i: (b * nt + i, 0)),
        out_shape=jax.ShapeDtypeStruct((batch * seq, MEM_WIDTH), BF16),
        compiler_params=_params("parallel", "parallel"),
        name="mem_attention",
    )(z, kv)


def _outproj_ln_kernel(mix_ref, mem_ref, w1_ref, w2_ref, x_ref, g_ref, b_ref, wr_ref, br_ref,
                       xo_ref, xt_ref, slab_ref):
    h = _dot(mix_ref[...], w1_ref[...]) + _dot(mem_ref[...], w2_ref[...])
    y = DEEPNORM_ALPHA * x_ref[...] + h
    out = _layer_norm_rows(y, g_ref[...], b_ref[...])
    xo_ref[...] = out
    _to_token_tiles(xt_ref, out)
    slab_ref[...] = _route(out, wr_ref[0], wr_ref[1], br_ref[...])


def _outproj_ln(mix, mem_out, w_mix, w_mem, x, g, b, w_router, b_router):
    n = x.shape[0]
    tm = _pick_tile(n, 512, 8)
    row = lambda i: (i, 0)
    fixed = lambda i: (0, 0)
    return pl.pallas_call(
        _outproj_ln_kernel,
        grid=(n // tm,),
        in_specs=[pl.BlockSpec((tm, MIX_WIDTH), row),
                  pl.BlockSpec((tm, MEM_WIDTH), row),
                  pl.BlockSpec((MIX_WIDTH, D_MODEL), fixed),
                  pl.BlockSpec((MEM_WIDTH, D_MODEL), fixed),
                  pl.BlockSpec((tm, D_MODEL), row),
                  pl.BlockSpec((1, D_MODEL), fixed),
                  pl.BlockSpec((1, D_MODEL), fixed),
                  pl.BlockSpec((2, D_MODEL, LANES), lambda i: (0, 0, 0)),
                  pl.BlockSpec((1, LANES), fixed)],
        out_specs=[pl.BlockSpec((tm, D_MODEL), row), pl.BlockSpec((tm * TOKEN_TILE_ROWS, LANES), row),
                   pl.BlockSpec((tm, LANES), row)],
        out_shape=[jax.ShapeDtypeStruct((n, D_MODEL), F32),
                   jax.ShapeDtypeStruct((n * TOKEN_TILE_ROWS, LANES), F32),
                   jax.ShapeDtypeStruct((n, LANES), F32)],
        compiler_params=_params("parallel"),
        name="outproj_ln",
    )(mix, mem_out, w_mix, w_mem, x, g, b, w_router, b_router)


def _conv_kernel(b_ref, c_ref, h_ref, w_ref, o_ref, carry_ref):
    @pl.when(pl.program_id(1) == 0)
    def _():
        carry_ref[...] = jnp.zeros_like(carry_ref)

    u = c_ref[...].astype(F32) * h_ref[...].astype(F32)
    tt = u.shape[0]
    rowid = lax.broadcasted_iota(jnp.int32, u.shape, 0)
    c0 = carry_ref[0:1, :]
    c1 = carry_ref[1:2, :]
    u1 = jnp.where(rowid == 0, c1, pltpu.roll(u, 1, axis=0))
    u2 = jnp.where(rowid == 0, c0, jnp.where(rowid == 1, c1, pltpu.roll(u, 2, axis=0)))
    carry_ref[0:2, :] = u[tt - 2:tt, :]
    y = w_ref[0:1, :] * u2 + w_ref[1:2, :] * u1 + w_ref[2:3, :] * u
    o_ref[...] = (b_ref[...].astype(F32) * y).astype(o_ref.dtype)


def _short_conv(z, conv_w, batch, seq):
    tt = _pick_tile(seq, 512, 8)
    nt = seq // tt
    spec = lambda c: pl.BlockSpec((tt, MIX_WIDTH), lambda b, i: (b * nt + i, c))
    return pl.pallas_call(
        _conv_kernel,
        grid=(batch, nt),
        in_specs=[spec(0), spec(1), spec(2),
                  pl.BlockSpec((3, MIX_WIDTH), lambda b, i: (0, 0))],
        out_specs=pl.BlockSpec((tt, MIX_WIDTH), lambda b, i: (b * nt + i, 0)),
        out_shape=jax.ShapeDtypeStruct((batch * seq, MIX_WIDTH), BF16),
        scratch_shapes=[pltpu.VMEM((8, MIX_WIDTH), F32)],
        compiler_params=_params("parallel", "arbitrary"),
        name="short_conv",
    )(z, z, z, conv_w)


def _sb_kernel(q_ref, k_ref, v_ref, suffix_ref, o_ref):
    i = pl.program_id(2)
    tq = q_ref.shape[0]
    bk = tq
    scale = math.log2(math.e) / math.sqrt(SB_HEAD_DIM)
    q = q_ref[...]
    suffix = suffix_ref[...]

    def rows_of(ref, j):
        return ref[pl.ds(pl.multiple_of(j * bk, bk), bk), :]

    def scores(j):
        return _dot_nt(q, rows_of(k_ref, j)) * scale

    def logs(s, valid):
        soft = jnp.log2(1.0 + jnp.exp2(-jnp.abs(s)))
        log_beta = jnp.minimum(s, 0.0) - soft
        log_not = log_beta - s
        if valid is not None:
            log_not = jnp.where(valid, log_not, 0.0)
        hi, lo = _split2(log_not)
        return log_beta, jnp.concatenate([hi, lo], axis=1), jnp.sum(log_not, axis=-1, keepdims=True)

    def weights(log_beta, within, later, valid):
        attn = jnp.exp2(log_beta + within + later)
        if valid is not None:
            attn = jnp.where(valid, attn, 0.0)
        return attn.astype(BF16)

    def blocks(js, later, valid):
        s = [scores(j) for j in js]
        lg = [logs(x, valid) for x in s]
        within = [_dot(x[1], suffix) for x in lg]
        pv = None
        for j, (log_beta, _, total), w in zip(js, lg, within):
            p = _dot(weights(log_beta, w, later, valid), rows_of(v_ref, j))
            pv = p if pv is None else pv + p
            later = later + total
        return pv, later

    valid = lax.broadcasted_iota(jnp.int32, (tq, bk), 1) < lax.broadcasted_iota(jnp.int32, (tq, bk), 0)
    acc, later = blocks([i], jnp.zeros((tq, 1), F32), valid)

    def group(width, first):
        def body(step, carry):
            acc, later = carry
            j = first - width * step
            pv, later = blocks([j - d for d in range(width)], later, None)
            return acc + pv, later
        return body

    n1 = i & 1
    n2 = lax.shift_right_logical(i, 1) & 1
    carry = lax.fori_loop(0, n1, group(1, i - 1), (acc, later))
    carry = lax.fori_loop(0, n2, group(2, i - 1 - n1), carry)
    acc, _ = lax.fori_loop(0, lax.shift_right_logical(i, 2), group(SB_GROUP, i - 1 - n1 - 2 * n2), carry)
    o_ref[...] = acc.astype(o_ref.dtype)


def _stick_breaking(z, batch, seq):
    tq = _pick_tile(seq, SB_Q_BLOCK, LANES)
    nq = seq // tq
    later_key = jnp.arange(tq)[:, None] > jnp.arange(tq)[None, :]
    suffix = jnp.concatenate([later_key, later_key], axis=0).astype(BF16)
    return pl.pallas_call(
        _sb_kernel,
        grid=(batch, SB_HEADS, nq),
        in_specs=[pl.BlockSpec((tq, SB_HEAD_DIM), lambda b, h, i: (b * nq + i, h)),
                  pl.BlockSpec((seq, SB_HEAD_DIM), lambda b, h, i: (b, SB_HEADS + h)),
                  pl.BlockSpec((seq, SB_HEAD_DIM), lambda b, h, i: (b, 2 * SB_HEADS + h)),
                  pl.BlockSpec((2 * tq, tq), lambda b, h, i: (0, 0))],
        out_specs=pl.BlockSpec((tq, SB_HEAD_DIM), lambda b, h, i: (b * nq + i, h)),
        out_shape=jax.ShapeDtypeStruct((batch * seq, MIX_WIDTH), BF16),
        compiler_params=_params("parallel", "parallel", "arbitrary"),
        name="stick_breaking",
    )(z, z, z, suffix)


def _rwkv_prep_kernel(zr_ref, zk_ref, zv_ref, zl_ref, mur_ref, muk_ref, muv_ref, mul_ref,
                      w2_ref, a2_ref, g2_ref, w0_ref, a0_ref, kk_ref, ka_ref, rk_ref,
                      seg_ref, segt_ref,
                      r_out, k_out, v_out, an_out, bb_out, lw_out, gate_out, bonus_out,
                      carry_ref):
    @pl.when(pl.program_id(2) == 0)
    def _():
        carry_ref[...] = jnp.zeros_like(carry_ref)

    def shifted(z_ref, slot, mu_ref):
        z = z_ref[...].astype(F32)
        tt = z.shape[0]
        rowid = lax.broadcasted_iota(jnp.int32, z.shape, 0)
        prev = jnp.where(rowid == 0, carry_ref[slot:slot + 1, :], pltpu.roll(z, 1, axis=0))
        carry_ref[slot:slot + 1, :] = z[tt - 1:tt, :]
        return z + (prev - z) * mu_ref[...]

    r = shifted(zr_ref, 0, mur_ref)
    k = shifted(zk_ref, 1, muk_ref)
    v = shifted(zv_ref, 2, muv_ref)
    zl = shifted(zl_ref, 3, mul_ref)
    zw = zl[:, 0:LANES]
    za = zl[:, LANES:2 * LANES]
    zg = zl[:, 2 * LANES:4 * LANES]

    dw = w0_ref[...] + _dot(jnp.tanh(zw).astype(BF16), w2_ref[...])
    neg = -dw
    softplus = jnp.maximum(neg, 0.0) + jnp.log(1.0 + jnp.exp(-jnp.abs(neg)))
    lw = -jnp.exp(-softplus - 0.5)
    a = jax.nn.sigmoid(a0_ref[...] + _dot(za.astype(BF16), a2_ref[...]))
    gate = _dot(jax.nn.sigmoid(zg).astype(BF16), g2_ref[...])

    seg = seg_ref[...]
    segt = segt_ref[...]
    kk = k * kk_ref[...]
    norm = jnp.sqrt(_dot_hl(_dot_hl(kk * kk, seg), segt))
    kkn = kk / jnp.maximum(norm, 1e-12)
    k2 = k * (1.0 + (a - 1.0) * ka_ref[...])
    bonus = _dot_hl(_dot_hl(r * k2 * rk_ref[...], seg), segt) * v

    r_out[...] = r.astype(r_out.dtype)
    k_out[...] = k2.astype(k_out.dtype)
    v_out[...] = v.astype(v_out.dtype)
    an_out[...] = (-kkn).astype(an_out.dtype)
    bb_out[...] = (kkn * a).astype(bb_out.dtype)
    lw_out[...] = lw
    gate_out[...] = gate.astype(gate_out.dtype)
    bonus_out[...] = bonus.astype(bonus_out.dtype)


def _rwkv_prep(z, mu_main, mu_lora, w2p, a2p, g2, w0, a0, k_k, k_a, r_k, batch, seq):
    n = batch * seq
    tt = _pick_tile(seq, 256, 8)
    nt = seq // tt
    nc = MIX_WIDTH // COL_BLOCK
    heads_per_block = COL_BLOCK // RWKV_HEAD_DIM
    ch = jnp.arange(COL_BLOCK)[:, None] // RWKV_HEAD_DIM
    seg = (ch == jnp.arange(LANES)[None, :]).astype(BF16)
    segt = seg.T
    assert heads_per_block <= LANES

    zspec = lambda off: pl.BlockSpec((tt, COL_BLOCK), lambda c, b, i: (b * nt + i, off + c))
    zlspec = pl.BlockSpec((tt, COL_BLOCK), lambda c, b, i: (b * nt + i, 3 * nc + 1))
    muspec = lambda off: pl.BlockSpec((1, COL_BLOCK), lambda c, b, i: (0, off + c))
    colspec = lambda rows: pl.BlockSpec((rows, COL_BLOCK), lambda c, b, i: (0, c))
    fixed = lambda shape: pl.BlockSpec(shape, lambda c, b, i: (0, 0))
    ospec = pl.BlockSpec((tt, COL_BLOCK), lambda c, b, i: (b * nt + i, c))
    outs = pl.pallas_call(
        _rwkv_prep_kernel,
        grid=(nc, batch, nt),
        in_specs=[zspec(0), zspec(nc), zspec(2 * nc), zlspec,
                  muspec(0), muspec(nc), muspec(2 * nc), fixed((1, COL_BLOCK)),
                  colspec(LANES), colspec(LANES), colspec(GATE_LORA),
                  colspec(1), colspec(1), colspec(1), colspec(1), colspec(1),
                  fixed((COL_BLOCK, LANES)), fixed((LANES, COL_BLOCK))],
        out_specs=[ospec] * 8,
        out_shape=[jax.ShapeDtypeStruct((n, MIX_WIDTH), F32 if name == "lw" else BF16)
                   for name in ("r", "k", "v", "an", "bb", "lw", "gate", "bonus")],
        scratch_shapes=[pltpu.VMEM((8, COL_BLOCK), F32)],
        compiler_params=_params("parallel", "parallel", "arbitrary"),
        name="rwkv_prep",
    )(z, z, z, z, mu_main, mu_main, mu_main, mu_lora, w2p, a2p, g2, w0, a0, k_k, k_a, r_k, seg, segt)
    return outs


def _each(fn, *cols):
    return [fn(*args) for args in zip(*cols)]


def _wkv_chunk_terms(chunks, consts):
    tri, lane_lo, strict, incl, blk16, eye = consts
    c = WKV_CHUNK
    n2 = 2 * c

    def pair_rows(x):
        return jnp.concatenate([jnp.where(lane_lo, x, 0.0), jnp.where(lane_lo, 0.0, x)], axis=0)

    def mm(x, y):
        return _dot(x.astype(BF16), y.astype(BF16))

    def cumulative(r, k, v, an, bb, lw):
        h1 = lw.astype(BF16)
        r1 = lw - h1.astype(F32)
        h2 = r1.astype(BF16)
        h3 = (r1 - h2.astype(F32)).astype(BF16)
        return _dot(tri, h1) + _dot(tri, h2) + _dot(tri, h3)

    cums = _each(cumulative, *zip(*chunks))

    def operands(chunk, cum):
        r, k, v, an, bb, lw = chunk
        cum_last = cum[c - 1:c, :]
        p_inv = jnp.exp(-cum)
        p_tail = jnp.exp(cum_last - cum)
        a_p = pair_rows(an * jnp.exp(cum - lw)).astype(BF16)
        q_p = pair_rows(r * jnp.exp(cum))
        b_p = pair_rows(bb * p_inv).astype(BF16)
        k_p = pair_rows(k * p_inv).astype(BF16)
        v_p = pair_rows(v).astype(BF16)
        bt_p = pair_rows(bb * p_tail).astype(BF16)
        kt_p = pair_rows(k * p_tail).astype(BF16)
        aq = jnp.concatenate([a_p, q_p.astype(BF16)], axis=0)
        bk = jnp.concatenate([b_p, k_p], axis=0)
        return a_p, q_p, v_p, jnp.concatenate([bt_p, kt_p], axis=0), aq, bk, jnp.exp(cum_last)

    a_p, q_p, v_p, btkt, aq, bk, pc = zip(*_each(operands, chunks, cums))
    g = _each(_dot_nt, aq, bk)
    l_ab = [jnp.where(strict, x[:n2, :n2], 0.0) for x in g]
    a_rb = [jnp.where(incl, x[n2:, :n2], 0.0).astype(BF16) for x in g]
    a_kk = [jnp.concatenate([jnp.where(strict, x[:n2, n2:], 0.0), jnp.where(incl, x[n2:, n2:], 0.0)],
                            axis=0).astype(BF16) for x in g]
    av = _each(_dot, a_kk, v_p)

    ld = [jnp.where(blk16, x, 0.0) for x in l_ab]
    lo = [x - y for x, y in zip(l_ab, ld)]
    l2 = _each(mm, ld, ld)
    l4 = _each(mm, l2, l2)
    l8 = _each(mm, l4, l4)
    dinv = _each(mm, [eye + x for x in ld], [eye + x for x in l2])
    dinv = _each(mm, dinv, [eye + x for x in l4])
    dinv = _each(mm, dinv, [eye + x for x in l8])
    m1 = _each(mm, dinv, lo)
    m2 = _each(mm, m1, m1)
    tinv = _each(mm, [eye + x for x in m1], [eye + x for x in m2])
    tinv = _each(mm, tinv, dinv)

    wu = _each(mm, tinv, [jnp.concatenate([a, x[:n2].astype(BF16)], axis=1) for a, x in zip(a_p, av)])
    wu_b = [x.astype(BF16) for x in wu]
    rb = _each(_dot, a_rb, wu_b)
    qw = [q + x[:, :n2] for q, x in zip(q_p, rb)]
    y0 = [x[:, n2:] + y[n2:] for x, y in zip(rb, av)]
    e = _each(_dot_tn, [x[:, :n2] for x in wu_b], [x[:n2] for x in btkt])
    f = _each(_dot_tn, [jnp.concatenate([x[:, n2:], y], axis=0) for x, y in zip(wu_b, v_p)], btkt)
    return [(a[:c] + a[c:], b[:c] + b[c:], ee, ff, p) for a, b, ee, ff, p in zip(qw, y0, e, f, pc)]


def _wkv_terms_kernel(r_ref, k_ref, v_ref, an_ref, bb_ref, lw_ref, qw_ref, y0_ref, e_ref, f_ref, pc_ref):
    c = WKV_CHUNK
    n2 = 2 * c
    row = lax.broadcasted_iota(jnp.int32, (n2, n2), 0)
    col = lax.broadcasted_iota(jnp.int32, (n2, n2), 1)
    tri_r = lax.broadcasted_iota(jnp.int32, (c, c), 0)
    tri_c = lax.broadcasted_iota(jnp.int32, (c, c), 1)
    consts = (
        jnp.where(tri_c <= tri_r, 1.0, 0.0).astype(BF16),
        lax.broadcasted_iota(jnp.int32, (c, WKV_PAIR), 1) < RWKV_HEAD_DIM,
        row > col,
        row >= col,
        (row >> 4) == (col >> 4),
        jnp.where(row == col, 1.0, 0.0),
    )
    n_chunks = r_ref.shape[0] // c
    chunks = []
    for j in range(n_chunks):
        rows = slice(j * c, (j + 1) * c)
        chunks.append(tuple(ref[rows, :].astype(F32) for ref in (r_ref, k_ref, v_ref, an_ref, bb_ref, lw_ref)))
    for j, (qw, y0, e, f, pc) in enumerate(_wkv_chunk_terms(chunks, consts)):
        rows = slice(j * c, (j + 1) * c)
        qw_ref[rows, :] = qw.astype(qw_ref.dtype)
        y0_ref[rows, :] = y0
        e_ref[j * n2:(j + 1) * n2, :] = e.astype(e_ref.dtype)
        f_ref[j * n2:(j + 1) * n2, :] = f
        pc_ref[j * 8:(j + 1) * 8, :] = jnp.broadcast_to(pc, (8, WKV_PAIR))


def _wkv_terms(r, k, v, an, bb, lw, batch, seq):
    n = batch * seq
    c = WKV_CHUNK
    n_pairs = MIX_WIDTH // WKV_PAIR
    cps = _pick_tile(seq // c, WKV_CHUNKS_PER_STEP, 1)
    steps = seq // (c * cps)
    tok = pl.BlockSpec((cps * c, WKV_PAIR), lambda b, p, i: (b * steps + i, p))
    mat = pl.BlockSpec((cps * WKV_PAIR, WKV_PAIR), lambda b, p, i: (b * steps + i, p))
    vec = pl.BlockSpec((cps * 8, WKV_PAIR), lambda b, p, i: (b * steps + i, p))
    n_chunks = n // c
    return pl.pallas_call(
        _wkv_terms_kernel,
        grid=(batch, n_pairs, steps),
        in_specs=[tok] * 6,
        out_specs=[tok, tok, mat, mat, vec],
        out_shape=[jax.ShapeDtypeStruct((n, MIX_WIDTH), BF16),
                   jax.ShapeDtypeStruct((n, MIX_WIDTH), F32),
                   jax.ShapeDtypeStruct((n_chunks * WKV_PAIR, MIX_WIDTH), BF16),
                   jax.ShapeDtypeStruct((n_chunks * WKV_PAIR, MIX_WIDTH), F32),
                   jax.ShapeDtypeStruct((n_chunks * 8, MIX_WIDTH), F32)],
        compiler_params=_params("parallel", "parallel", "parallel"),
        name="wkv_terms",
    )(r, k, v, an, bb, lw)


def _wkv_scan_kernel(qw_ref, y0_ref, e_ref, f_ref, pc_ref, bonus_ref, gate_ref, gng_ref, gnb_ref,
                     o_ref, state_ref):
    @pl.when(pl.program_id(2) == 0)
    def _():
        state_ref[...] = jnp.zeros_like(state_ref)

    n_pairs = state_ref.shape[1] // WKV_PAIR
    lane_lo = lax.broadcasted_iota(jnp.int32, (WKV_CHUNK, WKV_PAIR), 1) < RWKV_HEAD_DIM
    inv_n = 1.0 / RWKV_HEAD_DIM
    def head_mean(x):
        lo = jnp.sum(jnp.where(lane_lo, x, 0.0), axis=-1, keepdims=True)
        hi = jnp.sum(jnp.where(lane_lo, 0.0, x), axis=-1, keepdims=True)
        return jnp.where(lane_lo, lo, hi) * inv_n

    slices = [slice(p * WKV_PAIR, (p + 1) * WKV_PAIR) for p in range(n_pairs)]
    states = [state_ref[:, sl] for sl in slices]
    states_b = [s.astype(BF16) for s in states]
    ys = [_dot_nt(qw_ref[:, sl], sb) for sl, sb in zip(slices, states_b)]
    se = [_dot(sb, e_ref[:, sl]) for sl, sb in zip(slices, states_b)]
    for sl, s, x in zip(slices, states, se):
        state_ref[:, sl] = s * pc_ref[0:1, sl] + x + f_ref[:, sl]
    for sl, y in zip(slices, ys):
        y = y + y0_ref[:, sl]
        yc = y - head_mean(y)
        var = head_mean(yc * yc)
        yn = yc * lax.rsqrt(var + RWKV_GN_EPS) * gng_ref[:, sl] + gnb_ref[:, sl]
        o_ref[:, sl] = ((yn + bonus_ref[:, sl].astype(F32)) * gate_ref[:, sl].astype(F32)).astype(o_ref.dtype)


def _wkv_scan(qw, y0, e, f, pc, bonus, gate, gn_g, gn_b, batch, seq):
    n = batch * seq
    c = WKV_CHUNK
    nc = seq // c
    width = MIX_WIDTH // 2
    groups = MIX_WIDTH // width
    tok = pl.BlockSpec((c, width), lambda b, g, i: (b * nc + i, g))
    mat = pl.BlockSpec((WKV_PAIR, width), lambda b, g, i: (b * nc + i, g))
    vec = pl.BlockSpec((8, width), lambda b, g, i: (b * nc + i, g))
    par = pl.BlockSpec((1, width), lambda b, g, i: (0, g))
    return pl.pallas_call(
        _wkv_scan_kernel,
        grid=(batch, groups, nc),
        in_specs=[tok, tok, mat, mat, vec, tok, tok, par, par],
        out_specs=tok,
        out_shape=jax.ShapeDtypeStruct((n, MIX_WIDTH), BF16),
        scratch_shapes=[pltpu.VMEM((WKV_PAIR, width), F32)],
        compiler_params=_params("parallel", "parallel", "arbitrary"),
        name="wkv_scan",
    )(qw, y0, e, f, pc, bonus, gate, gn_g, gn_b)


def _route(x, w_hi, w_lo, b):
    x_hi, x_lo = _split2(x)
    logits = _dot(x_hi, w_hi) + (_dot(x_lo, w_hi) + _dot(x_hi, w_lo)) + b
    lane = lax.broadcasted_iota(jnp.int32, logits.shape, 1)
    neg_inf = -jnp.inf
    big = 4 * LANES

    def first_where(mask):
        return jnp.min(jnp.where(mask, lane, big), axis=-1, keepdims=True)

    gl = jnp.where(lane < N_GROUPS, logits, neg_inf)
    gmax = jnp.max(gl, axis=-1, keepdims=True)
    gidx = first_where(gl == gmax)
    p_group_sel = 1.0 / jnp.sum(jnp.exp(gl - gmax), axis=-1, keepdims=True)

    lo = N_GROUPS + EXPERTS_PER_GROUP * gidx
    emask = (lane >= lo) & (lane < lo + EXPERTS_PER_GROUP)
    el = jnp.where(emask, logits, neg_inf)
    emax = jnp.max(el, axis=-1, keepdims=True)
    pe = jnp.exp(el - emax)
    pe = pe / jnp.sum(pe, axis=-1, keepdims=True)
    pe = jnp.where(emask, pe, -1.0)
    p1 = jnp.max(pe, axis=-1, keepdims=True)
    i1 = first_where(pe == p1)
    pe2 = jnp.where(lane == i1, -1.0, pe)
    p2 = jnp.max(pe2, axis=-1, keepdims=True)
    i2 = first_where(pe2 == p2)
    denom = p1 + p2
    g1 = p_group_sel * p1 / denom
    g2 = p_group_sel * p2 / denom
    e1 = (i1 - N_GROUPS).astype(F32)
    e2 = (i2 - N_GROUPS).astype(F32)
    return jnp.where(lane == 0, e1, jnp.where(lane == 1, e2, jnp.where(lane == 2, g1, jnp.where(lane == 3, g2, 0.0))))


def _plan_kernel(e_ref, rank_ref, count_ref, carry_ref):
    @pl.when(pl.program_id(0) == 0)
    def _():
        carry_ref[...] = jnp.zeros_like(carry_ref)

    nb = e_ref.shape[-1]
    e = e_ref[0]
    onehot = jnp.where(lax.broadcasted_iota(jnp.int32, (N_EXPERTS, nb), 0) == e, 1.0, 0.0)
    r = lax.broadcasted_iota(jnp.int32, (nb, nb), 0)
    c = lax.broadcasted_iota(jnp.int32, (nb, nb), 1)
    before = jnp.where(r < c, 1.0, 0.0).astype(BF16)
    seen = _dot(onehot.astype(BF16), before) + carry_ref[:, 0:1]
    rank_ref[0] = jnp.sum(onehot * seen, axis=0, keepdims=True).astype(jnp.int32)
    carry_ref[...] = carry_ref[...] + jnp.sum(onehot, axis=1, keepdims=True)
    count_ref[...] = carry_ref[...]


def _dispatch_plan(e_flat):
    n_slots = e_flat.shape[0]
    nb = _pick_tile(n_slots, PLAN_BLOCK)
    steps = n_slots // nb
    rank, counts = pl.pallas_call(
        _plan_kernel,
        grid=(steps,),
        in_specs=[pl.BlockSpec((1, 1, nb), lambda i: (i, 0, 0))],
        out_specs=[pl.BlockSpec((1, 1, nb), lambda i: (i, 0, 0)),
                   pl.BlockSpec((N_EXPERTS, LANES), lambda i: (0, 0))],
        out_shape=[jax.ShapeDtypeStruct((steps, 1, nb), jnp.int32),
                   jax.ShapeDtypeStruct((N_EXPERTS, LANES), F32)],
        scratch_shapes=[pltpu.VMEM((N_EXPERTS, LANES), F32)],
        compiler_params=_params("arbitrary"),
        name="moe_plan",
    )(e_flat.reshape(steps, 1, nb))
    return rank.reshape(n_slots), counts[:, 0].astype(jnp.int32)


def _to_token_tiles(dst_ref, value):
    rows = value.shape[0]
    for j in range(TOKEN_TILE_ROWS):
        dst_ref[pl.ds(j, rows, stride=TOKEN_TILE_ROWS), :] = value[:, j * LANES:(j + 1) * LANES]


def _token_tile_columns(src_ref, rows, j):
    return src_ref[pl.ds(j, rows, stride=TOKEN_TILE_ROWS), :]


def _row_copy(table_hbm, dst_ref, src_row, dst_row, sem):
    src = pl.multiple_of(src_row * TOKEN_TILE_ROWS, TOKEN_TILE_ROWS)
    dst = pl.multiple_of(dst_row * TOKEN_TILE_ROWS, TOKEN_TILE_ROWS)
    return pltpu.make_async_copy(table_hbm.at[pl.ds(src, TOKEN_TILE_ROWS)],
                                 dst_ref.at[pl.ds(dst, TOKEN_TILE_ROWS)], sem)


def _token_gather(wait, idx_ref, offset, stride, n_tokens, table_hbm, dst_ref, sem):
    def group(g, carry):
        for u in range(GATHER_UNROLL):
            i = g * GATHER_UNROLL + u
            if wait:
                _row_copy(table_hbm, dst_ref, 0, i, sem).wait()
            else:
                _row_copy(table_hbm, dst_ref, idx_ref[0, 0, offset + stride * i], i, sem).start(priority=u % 2)
        return carry

    assert GATHER_UNROLL & (GATHER_UNROLL - 1) == 0
    groups = lax.shift_right_logical(jnp.asarray(n_tokens, jnp.int32) + (GATHER_UNROLL - 1),
                                     GATHER_UNROLL.bit_length() - 1)
    lax.fori_loop(0, groups, group, 0)


def _expert_kernel(be_ref, nu_ref, nv_ref, idx_ref, idx_next_ref, x_hbm, wg_ref, wu_ref, wd_ref, o_ref,
                   wg_s, wu_s, wd_s, xb_s, xbuf, sem):
    i = pl.program_id(0)
    n_used = nu_ref[0]
    slot = i & 1

    def gather(wait, refs, blk, slot):
        _token_gather(wait, refs, 0, 1, nv_ref[blk], x_hbm, xbuf.at[slot], sem.at[slot])

    @pl.when(i == 0)
    def _():
        xbuf[...] = jnp.zeros_like(xbuf)
        gather(False, idx_ref, 0, 0)

    @pl.when(i + 1 < n_used)
    def _():
        gather(False, idx_next_ref, i + 1, 1 - slot)

    prev = be_ref[jnp.maximum(i - 1, 0)]
    changed = (i == 0) | (be_ref[i] != prev)

    @pl.when(changed)
    def _():
        wg_s[...] = wg_ref[0, 0].astype(BF16)
        wu_s[...] = wu_ref[0, 0].astype(BF16)
        wd_s[...] = wd_ref[0, 0].astype(BF16)

    used = i < n_used

    @pl.when(used)
    def _():
        gather(True, idx_ref, i, slot)
        for j in range(TOKEN_TILE_ROWS):
            xb_s[:, j * LANES:(j + 1) * LANES] = _token_tile_columns(xbuf.at[slot], MOE_BLOCK, j).astype(BF16)
        parts = MOE_ROW_SPLIT
        rows = MOE_BLOCK // parts
        xs = [xb_s[p * rows:(p + 1) * rows, :] for p in range(parts)]
        g = [_dot(x, wg_s[...]) for x in xs]
        u = [_dot(x, wu_s[...]) for x in xs]
        h = [((a * jax.nn.sigmoid(a)) * b).astype(BF16) for a, b in zip(g, u)]
        y = [_dot(x, wd_s[...]) for x in h]
        for p in range(parts):
            _to_token_tiles(o_ref.at[pl.ds(p * rows * TOKEN_TILE_ROWS, rows * TOKEN_TILE_ROWS)], y[p])

    @pl.when(jnp.logical_not(used))
    def _():
        o_ref[...] = jnp.zeros_like(o_ref)


def _experts(x_tiles, src_tok, block_expert, n_used, n_valid, w_gate, w_up, w_down, layer):
    cap = src_tok.shape[0]
    n_blk = cap // MOE_BLOCK
    wspec = lambda shape: pl.BlockSpec((1, 1) + shape, lambda i, be, nu, nv: (layer, be[i], 0, 0))
    idx_spec = lambda f: pl.BlockSpec((1, 1, MOE_BLOCK), lambda i, be, nu, nv: (f(i), 0, 0),
                                      memory_space=pltpu.SMEM)
    grid_spec = pltpu.PrefetchScalarGridSpec(
        num_scalar_prefetch=3,
        grid=(n_blk,),
        in_specs=[idx_spec(lambda i: i), idx_spec(lambda i: jnp.minimum(i + 1, n_blk - 1)),
                  pl.BlockSpec(memory_space=pl.ANY),
                  wspec((D_MODEL, D_EXPERT)), wspec((D_MODEL, D_EXPERT)), wspec((D_EXPERT, D_MODEL))],
        out_specs=pl.BlockSpec((MOE_BLOCK * TOKEN_TILE_ROWS, LANES), lambda i, be, nu, nv: (i, 0)),
        scratch_shapes=[pltpu.VMEM((D_MODEL, D_EXPERT), BF16),
                        pltpu.VMEM((D_MODEL, D_EXPERT), BF16),
                        pltpu.VMEM((D_EXPERT, D_MODEL), BF16),
                        pltpu.VMEM((MOE_BLOCK, D_MODEL), BF16),
                        pltpu.VMEM((2, MOE_BLOCK * TOKEN_TILE_ROWS, LANES), F32),
                        pltpu.SemaphoreType.DMA((2,))],
    )
    idx = src_tok.reshape(n_blk, 1, MOE_BLOCK)
    return pl.pallas_call(
        _expert_kernel,
        grid_spec=grid_spec,
        out_shape=jax.ShapeDtypeStruct((cap * TOKEN_TILE_ROWS, LANES), F32),
        compiler_params=_params("arbitrary"),
        name="moe_experts",
    )(block_expert, n_used, n_valid, idx, idx, x_tiles, w_gate, w_up, w_down)


def _combine_ln_kernel(idx_ref, idx_next_ref, ys_hbm, slab_ref, x_ref, g_ref, b_ref, xo_ref, xb_ref,
                       buf, y_s, sem):
    i = pl.program_id(0)
    slot = i & 1
    tm = x_ref.shape[0]

    def gather(wait, refs, slot):
        for k in range(TOP_K):
            _token_gather(wait, refs, k, TOP_K, tm, ys_hbm, buf.at[slot, k], sem.at[slot, k])

    @pl.when(i == 0)
    def _():
        gather(False, idx_ref, 0)

    @pl.when(i + 1 < pl.num_programs(0))
    def _():
        gather(False, idx_next_ref, 1 - slot)

    slab = slab_ref[...]
    lane = lax.broadcasted_iota(jnp.int32, slab.shape, 1)
    g1 = jnp.sum(jnp.where(lane == 2, slab, 0.0), axis=-1, keepdims=True)
    g2 = jnp.sum(jnp.where(lane == 3, slab, 0.0), axis=-1, keepdims=True)
    gather(True, idx_ref, slot)
    for j in range(TOKEN_TILE_ROWS):
        cols = slice(j * LANES, (j + 1) * LANES)
        f = (_token_tile_columns(buf.at[slot, 0], tm, j) * g1
             + _token_tile_columns(buf.at[slot, 1], tm, j) * g2)
        y_s[:, cols] = DEEPNORM_ALPHA * x_ref[:, cols] + f
    out = _layer_norm_rows(y_s[...], g_ref[...], b_ref[...])
    xo_ref[...] = out
    xb_ref[...] = out.astype(BF16)


def _combine_ln(ys, dest, slab, x, g, b):
    n = x.shape[0]
    tm = _pick_tile(n, 256, 8)
    steps = n // tm
    row = lambda i: (i, 0)
    fixed = lambda i: (0, 0)
    idx_spec = lambda f: pl.BlockSpec((1, 1, TOP_K * tm), lambda i: (f(i), 0, 0), memory_space=pltpu.SMEM)
    idx = dest.reshape(steps, 1, TOP_K * tm)
    return pl.pallas_call(
        _combine_ln_kernel,
        grid=(steps,),
        in_specs=[idx_spec(lambda i: i), idx_spec(lambda i: jnp.minimum(i + 1, steps - 1)),
                  pl.BlockSpec(memory_space=pl.ANY),
                  pl.BlockSpec((tm, LANES), row),
                  pl.BlockSpec((tm, D_MODEL), row),
                  pl.BlockSpec((1, D_MODEL), fixed),
                  pl.BlockSpec((1, D_MODEL), fixed)],
        out_specs=[pl.BlockSpec((tm, D_MODEL), row), pl.BlockSpec((tm, D_MODEL), row)],
        out_shape=[jax.ShapeDtypeStruct((n, D_MODEL), F32), jax.ShapeDtypeStruct((n, D_MODEL), BF16)],
        scratch_shapes=[pltpu.VMEM((2, TOP_K, tm * TOKEN_TILE_ROWS, LANES), F32),
                        pltpu.VMEM((tm, D_MODEL), F32),
                        pltpu.SemaphoreType.DMA((2, TOP_K))],
        compiler_params=_params("arbitrary"),
        name="moe_combine_ln",
    )(idx, idx, ys, slab, x, g, b)


def _router_params(rg_w, rg_b, re_w, re_b):
    pad_cols = LANES - N_GROUPS - N_EXPERTS
    w_router = jnp.concatenate([rg_w, re_w, jnp.zeros((D_MODEL, pad_cols), F32)], axis=1)
    b_router = jnp.concatenate([rg_b, re_b, jnp.zeros((pad_cols,), F32)])[None, :]
    w_hi = w_router.astype(BF16)
    w_lo = (w_router - w_hi.astype(F32)).astype(BF16)
    return jnp.stack([w_hi, w_lo]), b_router


def _moe_ln(x, x_tiles, slab, w_gate, w_up, w_down, layer, ln_g, ln_b):
    n = x.shape[0]
    n_slots = n * TOP_K

    e_flat = slab[:, :TOP_K].astype(jnp.int32).reshape(n_slots)
    rank, counts = _dispatch_plan(e_flat)
    padded = (counts + MOE_BLOCK - 1) // MOE_BLOCK * MOE_BLOCK
    pad_end = jnp.cumsum(padded)
    pad_start = pad_end - padded
    dest = pad_start[e_flat] + rank
    cap = n_slots + N_EXPERTS * MOE_BLOCK
    n_blk = cap // MOE_BLOCK
    src_tok = jnp.zeros((cap,), jnp.int32).at[dest].set(jnp.arange(n_slots, dtype=jnp.int32) // TOP_K)
    block_start = jnp.arange(n_blk, dtype=jnp.int32) * MOE_BLOCK
    block_expert = jnp.minimum(jnp.sum((pad_end[None, :] <= block_start[:, None]).astype(jnp.int32), axis=1),
                               N_EXPERTS - 1)
    n_used = (pad_end[-1:] // MOE_BLOCK).astype(jnp.int32)
    n_valid = jnp.clip(counts[block_expert] - (block_start - pad_start[block_expert]), 0, MOE_BLOCK)
    n_valid = jnp.where(block_start < pad_end[-1], n_valid, 0).astype(jnp.int32)

    ys = _experts(x_tiles, src_tok, block_expert, n_used, n_valid, w_gate, w_up, w_down, layer)
    return _combine_ln(ys, dest, slab, x, ln_g[None, :], ln_b[None, :])


def _rwkv_mixer(x_bf, w_in, mu, w0, w2, a0, a2, g2, k_k, k_a, r_k, gn_g, gn_b, batch, seq):
    c3 = 3 * MIX_WIDTH
    lo_w, lo_a, lo_g = c3, c3 + DECAY_LORA, c3 + DECAY_LORA + AAA_LORA
    pad = LANES - DECAY_LORA

    def permute_cols(t, with_mem):
        zeros = jnp.zeros(t.shape[:-1] + (pad,), t.dtype)
        parts = [t[..., :c3]]
        if with_mem:
            parts.append(t[..., RWKV_COLS:])
        parts += [t[..., lo_w:lo_a], zeros, t[..., lo_a:lo_g], zeros, t[..., lo_g:RWKV_COLS]]
        return jnp.concatenate(parts, axis=-1)

    w_perm = permute_cols(w_in, True).astype(BF16)
    mu_perm = permute_cols(mu[None, :], False)
    mu_main, mu_lora = mu_perm[:, :c3], mu_perm[:, c3:]
    z = _matmul(x_bf, w_perm, BF16, "rwkv_in_proj")

    zrow = jnp.zeros((pad, MIX_WIDTH), F32)
    w2p = jnp.concatenate([w2, zrow], axis=0).astype(BF16)
    a2p = jnp.concatenate([a2, zrow], axis=0).astype(BF16)
    r, k, v, an, bb, lw, gate, bonus = _rwkv_prep(
        z, mu_main, mu_lora, w2p, a2p, g2.astype(BF16), w0[None, :], a0[None, :], k_k[None, :], k_a[None, :],
        r_k.reshape(1, MIX_WIDTH), batch, seq)
    qw, y0, e, f, pc = _wkv_terms(r, k, v, an, bb, lw, batch, seq)
    mix = _wkv_scan(qw, y0, e, f, pc, bonus, gate, gn_g[None, :], gn_b[None, :], batch, seq)
    return z, mix, c3 // COL_BLOCK


def kernel(x, mem, w_out, mem_wk, mem_wv, ln_mix_g, ln_mix_b, ln_ffn_g, ln_ffn_b, router_group_w, router_group_b, router_expert_w, router_expert_b, expert_w_gate, expert_w_up, expert_w_down, rwkv_w_in, rwkv_mu, rwkv_w0, rwkv_w2, rwkv_a0, rwkv_a2, rwkv_g2, rwkv_k_k, rwkv_k_a, rwkv_r_k, rwkv_gn_g, rwkv_gn_b, conv_w_in, conv_w, sb_w_in):
    batch, seq, d = x.shape
    assert d == D_MODEL and mem.shape == (batch, MEM_LEN, D_MODEL)
    n = batch * seq
    xf = x.reshape(n, D_MODEL)
    xb = xf.astype(BF16)
    mem_bf = mem.reshape(batch * MEM_LEN, D_MODEL).astype(BF16)
    q_block = 3 * MIX_WIDTH // COL_BLOCK

    for i in range(DEPTH):
        kind, j = i % N_MIXERS, i // N_MIXERS
        if kind == 0:
            z, mix, _ = _rwkv_mixer(xb, rwkv_w_in[j], rwkv_mu[j], rwkv_w0[j], rwkv_w2[j], rwkv_a0[j], rwkv_a2[j],
                                    rwkv_g2[j], rwkv_k_k[j], rwkv_k_a[j], rwkv_r_k[j], rwkv_gn_g[j], rwkv_gn_b[j],
                                    batch, seq)
        elif kind == 1:
            z = _matmul(xb, conv_w_in[j].astype(BF16), BF16, "conv_in_proj")
            mix = _short_conv(z, conv_w[j], batch, seq)
        else:
            z = _matmul(xb, sb_w_in[j].astype(BF16), BF16, "sb_in_proj")
            mix = _stick_breaking(z, batch, seq)
        w_kv = jnp.concatenate([mem_wk[i], mem_wv[i]], axis=1).astype(BF16)
        kv = _matmul(mem_bf, w_kv, BF16, "mem_kv_proj")
        mem_out = _mem_attention(z, kv, batch, seq, q_block)
        w_o = w_out[i].astype(BF16)
        w_router, b_router = _router_params(router_group_w[i], router_group_b[i],
                                            router_expert_w[i], router_expert_b[i])
        xf, x_tiles, slab = _outproj_ln(mix, mem_out, w_o[:MIX_WIDTH], w_o[MIX_WIDTH:], xf,
                                        ln_mix_g[i][None, :], ln_mix_b[i][None, :], w_router, b_router)
        xf, xb = _moe_ln(xf, x_tiles, slab, expert_w_gate, expert_w_up, expert_w_down, i,
                         ln_ffn_g[i], ln_ffn_b[i])
    return xf.reshape(batch, seq, D_MODEL)
```

```python
import functools
import math

import jax
import jax.numpy as jnp
from jax import lax
from jax.experimental import pallas as pl
from jax.experimental.pallas import tpu as pltpu

F32 = jnp.float32
BF16 = jnp.bfloat16

D_MODEL = 2048
DEPTH = 4
N_MIXERS = 3
MEM_LEN = 256
MIX_WIDTH = 1536
MEM_HEADS = 4
MEM_HEAD_DIM = 128
MEM_WIDTH = 512
RWKV_HEAD_DIM = 64
RWKV_HEADS = 24
DECAY_LORA = 96
AAA_LORA = 96
GATE_LORA = 256
RWKV_GN_EPS = 64e-5
RWKV_COLS = 3 * MIX_WIDTH + DECAY_LORA + AAA_LORA + GATE_LORA
SB_HEAD_DIM = 128
SB_HEADS = 12
N_GROUPS = 4
EXPERTS_PER_GROUP = 8
N_EXPERTS = 32
TOP_K = 2
D_EXPERT = 512
DEEPNORM_ALPHA = (2 * DEPTH) ** 0.25
LN_EPS = 1e-5

LANES = 128
V7X_VMEM_LIMIT_BYTES = 56 << 20

COL_BLOCK = 512
MOE_BLOCK = 256
MOE_ROW_SPLIT = 2
PLAN_BLOCK = 512
TOKEN_TILE_ROWS = D_MODEL // LANES
WKV_CHUNK = 64
WKV_PAIR = 2 * RWKV_HEAD_DIM
WKV_CHUNKS_PER_STEP = 8
WKV_SCAN_CHUNKS_PER_STEP = 4
SB_Q_BLOCK = 256
SB_GROUP = 4


def _params(*sem):
    return pltpu.CompilerParams(dimension_semantics=sem, vmem_limit_bytes=V7X_VMEM_LIMIT_BYTES)


def _dot(a, b):
    return jnp.dot(a, b, preferred_element_type=F32)


def _dot_nt(a, b):
    return lax.dot_general(a, b, (((1,), (1,)), ((), ())), preferred_element_type=F32)


def _dot_tn(a, b):
    return lax.dot_general(a, b, (((0,), (0,)), ((), ())), preferred_element_type=F32)


def _split2(x):
    hi = x.astype(BF16)
    lo = (x - hi.astype(F32)).astype(BF16)
    return hi, lo


def _dot_hl(x, w):
    hi, lo = _split2(x)
    return _dot(hi, w) + _dot(lo, w)


def _pick_tile(n, target, quantum=LANES):
    best = None
    for t in range(quantum, min(n, target) + 1, quantum):
        if n % t == 0:
            best = t
    assert best is not None, (n, target)
    return best


def _layer_norm_rows(y, g, b):
    mu = jnp.mean(y, axis=-1, keepdims=True)
    yc = y - mu
    var = jnp.mean(yc * yc, axis=-1, keepdims=True)
    return yc * lax.rsqrt(var + LN_EPS) * g + b


def _matmul_kernel(x_ref, w_ref, o_ref):
    o_ref[...] = _dot(x_ref[...], w_ref[...]).astype(o_ref.dtype)


def _matmul(x, w, out_dtype, name):
    m, k = x.shape
    n = w.shape[1]
    tm = _pick_tile(m, 1024, 8)
    tn = _pick_tile(n, 1536)
    return pl.pallas_call(
        _matmul_kernel,
        grid=(n // tn, m // tm),
        in_specs=[pl.BlockSpec((tm, k), lambda j, i: (i, 0)),
                  pl.BlockSpec((k, tn), lambda j, i: (0, j))],
        out_specs=pl.BlockSpec((tm, tn), lambda j, i: (i, j)),
        out_shape=jax.ShapeDtypeStruct((m, n), out_dtype),
        compiler_params=_params("parallel", "arbitrary"),
        name=name,
    )(x, w)


def _mem_attn_kernel(q_ref, kv_ref, o_ref):
    scale = 1.0 / math.sqrt(MEM_HEAD_DIM)
    for h in range(MEM_HEADS):
        sl = slice(h * MEM_HEAD_DIM, (h + 1) * MEM_HEAD_DIM)
        q = q_ref[:, sl]
        km = kv_ref[:, sl]
        vm = kv_ref[:, MEM_WIDTH + h * MEM_HEAD_DIM:MEM_WIDTH + (h + 1) * MEM_HEAD_DIM]
        s = _dot_nt(q, km) * scale
        s = s - jnp.max(s, axis=-1, keepdims=True)
        p = jnp.exp(s)
        p = p / jnp.sum(p, axis=-1, keepdims=True)
        o_ref[:, sl] = _dot(p.astype(BF16), vm).astype(o_ref.dtype)


def _mem_attention(z, kv, batch, seq, q_col_block):
    tq = _pick_tile(seq, 512, 8)
    nt = seq // tq
    return pl.pallas_call(
        _mem_attn_kernel,
        grid=(batch, nt),
        in_specs=[pl.BlockSpec((tq, MEM_WIDTH), lambda b, i: (b * nt + i, q_col_block)),
                  pl.BlockSpec((MEM_LEN, 2 * MEM_WIDTH), lambda b, i: (b, 0))],
        out_specs=pl.BlockSpec((tq, MEM_WIDTH), lambda b, i: (b * nt + i, 0)),
        out_shape=jax.ShapeDtypeStruct((batch * seq, MEM_WIDTH), BF16),
        compiler_params=_params("parallel", "parallel"),
        name="mem_attention",
    )(z, kv)


def _outproj_ln_kernel(mix_ref, mem_ref, w1_ref, w2_ref, x_ref, g_ref, b_ref, wr_ref, br_ref,
                       xo_ref, xt_ref, slab_ref):
    h = _dot(mix_ref[...], w1_ref[...]) + _dot(mem_ref[...], w2_ref[...])
    y = DEEPNORM_ALPHA * x_ref[...] + h
    out = _layer_norm_rows(y, g_ref[...], b_ref[...])
    xo_ref[...] = out
    _to_token_tiles(xt_ref, out)
    slab_ref[...] = _route(out, wr_ref[0], wr_ref[1], br_ref[...])


def _outproj_ln(mix, mem_out, w_mix, w_mem, x, g, b, w_router, b_router):
    n = x.shape[0]
    tm = _pick_tile(n, 512, 8)
    row = lambda i: (i, 0)
    fixed = lambda i: (0, 0)
    return pl.pallas_call(
        _outproj_ln_kernel,
        grid=(n // tm,),
        in_specs=[pl.BlockSpec((tm, MIX_WIDTH), row),
                  pl.BlockSpec((tm, MEM_WIDTH), row),
                  pl.BlockSpec((MIX_WIDTH, D_MODEL), fixed),
                  pl.BlockSpec((MEM_WIDTH, D_MODEL), fixed),
                  pl.BlockSpec((tm, D_MODEL), row),
                  pl.BlockSpec((1, D_MODEL), fixed),
                  pl.BlockSpec((1, D_MODEL), fixed),
                  pl.BlockSpec((2, D_MODEL, LANES), lambda i: (0, 0, 0)),
                  pl.BlockSpec((1, LANES), fixed)],
        out_specs=[pl.BlockSpec((tm, D_MODEL), row), pl.BlockSpec((tm * TOKEN_TILE_ROWS, LANES), row),
                   pl.BlockSpec((tm, LANES), row)],
        out_shape=[jax.ShapeDtypeStruct((n, D_MODEL), F32),
                   jax.ShapeDtypeStruct((n * TOKEN_TILE_ROWS, LANES), F32),
                   jax.ShapeDtypeStruct((n, LANES), F32)],
        compiler_params=_params("parallel"),
        name="outproj_ln",
    )(mix, mem_out, w_mix, w_mem, x, g, b, w_router, b_router)


def _conv_kernel(b_ref, c_ref, h_ref, w_ref, o_ref, carry_ref):
    @pl.when(pl.program_id(1) == 0)
    def _():
        carry_ref[...] = jnp.zeros_like(carry_ref)

    u = c_ref[...].astype(F32) * h_ref[...].astype(F32)
    tt = u.shape[0]
    rowid = lax.broadcasted_iota(jnp.int32, u.shape, 0)
    c0 = carry_ref[0:1, :]
    c1 = carry_ref[1:2, :]
    u1 = jnp.where(rowid == 0, c1, pltpu.roll(u, 1, axis=0))
    u2 = jnp.where(rowid == 0, c0, jnp.where(rowid == 1, c1, pltpu.roll(u, 2, axis=0)))
    carry_ref[0:2, :] = u[tt - 2:tt, :]
    y = w_ref[0:1, :] * u2 + w_ref[1:2, :] * u1 + w_ref[2:3, :] * u
    o_ref[...] = (b_ref[...].astype(F32) * y).astype(o_ref.dtype)


def _short_conv(z, conv_w, batch, seq):
    tt = _pick_tile(seq, 512, 8)
    nt = seq // tt
    spec = lambda c: pl.BlockSpec((tt, MIX_WIDTH), lambda b, i: (b * nt + i, c))
    return pl.pallas_call(
        _conv_kernel,
        grid=(batch, nt),
        in_specs=[spec(0), spec(1), spec(2),
                  pl.BlockSpec((3, MIX_WIDTH), lambda b, i: (0, 0))],
        out_specs=pl.BlockSpec((tt, MIX_WIDTH), lambda b, i: (b * nt + i, 0)),
        out_shape=jax.ShapeDtypeStruct((batch * seq, MIX_WIDTH), BF16),
        scratch_shapes=[pltpu.VMEM((8, MIX_WIDTH), F32)],
        compiler_params=_params("parallel", "arbitrary"),
        name="short_conv",
    )(z, z, z, conv_w)


def _sb_kernel(q_ref, k_ref, v_ref, suffix_ref, o_ref):
    i = pl.program_id(2)
    tq = q_ref.shape[0]
    bk = tq
    scale = math.log2(math.e) / math.sqrt(SB_HEAD_DIM)
    q = q_ref[...]
    suffix = suffix_ref[...]

    def rows_of(ref, j):
        return ref[pl.ds(pl.multiple_of(j * bk, bk), bk), :]

    def scores(j):
        return _dot_nt(q, rows_of(k_ref, j)) * scale

    def logs(s, valid):
        soft = jnp.log2(1.0 + jnp.exp2(-jnp.abs(s)))
        log_beta = jnp.minimum(s, 0.0) - soft
        log_not = log_beta - s
        if valid is not None:
            log_not = jnp.where(valid, log_not, 0.0)
        hi, lo = _split2(log_not)
        return log_beta, jnp.concatenate([hi, lo], axis=1), jnp.sum(log_not, axis=-1, keepdims=True)

    def weights(log_beta, within, later, valid):
        attn = jnp.exp2(log_beta + within + later)
        if valid is not None:
            attn = jnp.where(valid, attn, 0.0)
        return attn.astype(BF16)

    def blocks(js, later, valid):
        s = [scores(j) for j in js]
        lg = [logs(x, valid) for x in s]
        within = [_dot(x[1], suffix) for x in lg]
        pv = None
        for j, (log_beta, _, total), w in zip(js, lg, within):
            p = _dot(weights(log_beta, w, later, valid), rows_of(v_ref, j))
            pv = p if pv is None else pv + p
            later = later + total
        return pv, later

    valid = lax.broadcasted_iota(jnp.int32, (tq, bk), 1) < lax.broadcasted_iota(jnp.int32, (tq, bk), 0)
    acc, later = blocks([i], jnp.zeros((tq, 1), F32), valid)

    def group(width, first):
        def body(step, carry):
            acc, later = carry
            j = first - width * step
            pv, later = blocks([j - d for d in range(width)], later, None)
            return acc + pv, later
        return body

    n1 = i & 1
    n2 = lax.shift_right_logical(i, 1) & 1
    carry = lax.fori_loop(0, n1, group(1, i - 1), (acc, later))
    carry = lax.fori_loop(0, n2, group(2, i - 1 - n1), carry)
    acc, _ = lax.fori_loop(0, lax.shift_right_logical(i, 2), group(SB_GROUP, i - 1 - n1 - 2 * n2), carry)
    o_ref[...] = acc.astype(o_ref.dtype)


def _stick_breaking(z, batch, seq):
    tq = _pick_tile(seq, SB_Q_BLOCK, LANES)
    nq = seq // tq
    later_key = jnp.arange(tq)[:, None] > jnp.arange(tq)[None, :]
    suffix = jnp.concatenate([later_key, later_key], axis=0).astype(BF16)
    return pl.pallas_call(
        _sb_kernel,
        grid=(batch, SB_HEADS, nq),
        in_specs=[pl.BlockSpec((tq, SB_HEAD_DIM), lambda b, h, i: (b * nq + i, h)),
                  pl.BlockSpec((seq, SB_HEAD_DIM), lambda b, h, i: (b, SB_HEADS + h)),
                  pl.BlockSpec((seq, SB_HEAD_DIM), lambda b, h, i: (b, 2 * SB_HEADS + h)),
                  pl.BlockSpec((2 * tq, tq), lambda b, h, i: (0, 0))],
        out_specs=pl.BlockSpec((tq, SB_HEAD_DIM), lambda b, h, i: (b * nq + i, h)),
        out_shape=jax.ShapeDtypeStruct((batch * seq, MIX_WIDTH), BF16),
        compiler_params=_params("parallel", "parallel", "arbitrary"),
        name="stick_breaking",
    )(z, z, z, suffix)


def _rwkv_prep_kernel(zr_ref, zk_ref, zv_ref, zl_ref, mur_ref, muk_ref, muv_ref, mul_ref,
                      w2_ref, a2_ref, g2_ref, w0_ref, a0_ref, kk_ref, ka_ref, rk_ref,
                      seg_ref, segt_ref,
                      r_out, k_out, v_out, an_out, bb_out, lw_out, gate_out, bonus_out,
                      carry_ref):
    @pl.when(pl.program_id(2) == 0)
    def _():
        carry_ref[...] = jnp.zeros_like(carry_ref)

    def shifted(z_ref, slot, mu_ref):
        z = z_ref[...].astype(F32)
        tt = z.shape[0]
        rowid = lax.broadcasted_iota(jnp.int32, z.shape, 0)
        prev = jnp.where(rowid == 0, carry_ref[slot:slot + 1, :], pltpu.roll(z, 1, axis=0))
        carry_ref[slot:slot + 1, :] = z[tt - 1:tt, :]
        return z + (prev - z) * mu_ref[...]

    r = shifted(zr_ref, 0, mur_ref)
    k = shifted(zk_ref, 1, muk_ref)
    v = shifted(zv_ref, 2, muv_ref)
    zl = shifted(zl_ref, 3, mul_ref)
    zw = zl[:, 0:LANES]
    za = zl[:, LANES:2 * LANES]
    zg = zl[:, 2 * LANES:4 * LANES]

    dw = w0_ref[...] + _dot(jnp.tanh(zw).astype(BF16), w2_ref[...])
    neg = -dw
    softplus = jnp.maximum(neg, 0.0) + jnp.log(1.0 + jnp.exp(-jnp.abs(neg)))
    lw = -jnp.exp(-softplus - 0.5)
    a = jax.nn.sigmoid(a0_ref[...] + _dot(za.astype(BF16), a2_ref[...]))
    gate = _dot(jax.nn.sigmoid(zg).astype(BF16), g2_ref[...])

    seg = seg_ref[...]
    segt = segt_ref[...]
    kk = k * kk_ref[...]
    norm = jnp.sqrt(_dot_hl(_dot_hl(kk * kk, seg), segt))
    kkn = kk / jnp.maximum(norm, 1e-12)
    k2 = k * (1.0 + (a - 1.0) * ka_ref[...])
    bonus = _dot_hl(_dot_hl(r * k2 * rk_ref[...], seg), segt) * v

    r_out[...] = r.astype(r_out.dtype)
    k_out[...] = k2.astype(k_out.dtype)
    v_out[...] = v.astype(v_out.dtype)
    an_out[...] = (-kkn).astype(an_out.dtype)
    bb_out[...] = (kkn * a).astype(bb_out.dtype)
    lw_out[...] = lw
    gate_out[...] = gate.astype(gate_out.dtype)
    bonus_out[...] = bonus.astype(bonus_out.dtype)


def _rwkv_prep(z, mu_main, mu_lora, w2p, a2p, g2, w0, a0, k_k, k_a, r_k, batch, seq):
    n = batch * seq
    tt = _pick_tile(seq, 256, 8)
    nt = seq // tt
    nc = MIX_WIDTH // COL_BLOCK
    heads_per_block = COL_BLOCK // RWKV_HEAD_DIM
    ch = jnp.arange(COL_BLOCK)[:, None] // RWKV_HEAD_DIM
    seg = (ch == jnp.arange(LANES)[None, :]).astype(BF16)
    segt = seg.T
    assert heads_per_block <= LANES

    zspec = lambda off: pl.BlockSpec((tt, COL_BLOCK), lambda c, b, i: (b * nt + i, off + c))
    zlspec = pl.BlockSpec((tt, COL_BLOCK), lambda c, b, i: (b * nt + i, 3 * nc + 1))
    muspec = lambda off: pl.BlockSpec((1, COL_BLOCK), lambda c, b, i: (0, off + c))
    colspec = lambda rows: pl.BlockSpec((rows, COL_BLOCK), lambda c, b, i: (0, c))
    fixed = lambda shape: pl.BlockSpec(shape, lambda c, b, i: (0, 0))
    ospec = pl.BlockSpec((tt, COL_BLOCK), lambda c, b, i: (b * nt + i, c))
    outs = pl.pallas_call(
        _rwkv_prep_kernel,
        grid=(nc, batch, nt),
        in_specs=[zspec(0), zspec(nc), zspec(2 * nc), zlspec,
                  muspec(0), muspec(nc), muspec(2 * nc), fixed((1, COL_BLOCK)),
                  colspec(LANES), colspec(LANES), colspec(GATE_LORA),
                  colspec(1), colspec(1), colspec(1), colspec(1), colspec(1),
                  fixed((COL_BLOCK, LANES)), fixed((LANES, COL_BLOCK))],
        out_specs=[ospec] * 8,
        out_shape=[jax.ShapeDtypeStruct((n, MIX_WIDTH), F32 if name == "lw" else BF16)
                   for name in ("r", "k", "v", "an", "bb", "lw", "gate", "bonus")],
        scratch_shapes=[pltpu.VMEM((8, COL_BLOCK), F32)],
        compiler_params=_params("parallel", "parallel", "arbitrary"),
        name="rwkv_prep",
    )(z, z, z, z, mu_main, mu_main, mu_main, mu_lora, w2p, a2p, g2, w0, a0, k_k, k_a, r_k, seg, segt)
    return outs


def _each(fn, *cols):
    return [fn(*args) for args in zip(*cols)]


def _wkv_chunk_terms(chunks, consts):
    tri, lane_lo, strict, incl, blk16, eye = consts
    c = WKV_CHUNK
    n2 = 2 * c

    def pair_rows(x):
        return jnp.concatenate([jnp.where(lane_lo, x, 0.0), jnp.where(lane_lo, 0.0, x)], axis=0)

    def mm(x, y):
        return _dot(x.astype(BF16), y.astype(BF16))

    def cumulative(r, k, v, an, bb, lw):
        h1 = lw.astype(BF16)
        r1 = lw - h1.astype(F32)
        h2 = r1.astype(BF16)
        h3 = (r1 - h2.astype(F32)).astype(BF16)
        return _dot(tri, h1) + _dot(tri, h2) + _dot(tri, h3)

    cums = _each(cumulative, *zip(*chunks))

    def operands(chunk, cum):
        r, k, v, an, bb, lw = chunk
        cum_last = cum[c - 1:c, :]
        p_inv = jnp.exp(-cum)
        p_tail = jnp.exp(cum_last - cum)
        a_p = pair_rows(an * jnp.exp(cum - lw)).astype(BF16)
        q_p = pair_rows(r * jnp.exp(cum))
        b_p = pair_rows(bb * p_inv).astype(BF16)
        k_p = pair_rows(k * p_inv).astype(BF16)
        v_p = pair_rows(v).astype(BF16)
        bt_p = pair_rows(bb * p_tail).astype(BF16)
        kt_p = pair_rows(k * p_tail).astype(BF16)
        aq = jnp.concatenate([a_p, q_p.astype(BF16)], axis=0)
        bk = jnp.concatenate([b_p, k_p], axis=0)
        return a_p, q_p, v_p, jnp.concatenate([bt_p, kt_p], axis=0), aq, bk, jnp.exp(cum_last)

    a_p, q_p, v_p, btkt, aq, bk, pc = zip(*_each(operands, chunks, cums))
    g = _each(_dot_nt, aq, bk)
    l_ab = [jnp.where(strict, x[:n2, :n2], 0.0) for x in g]
    a_rb = [jnp.where(incl, x[n2:, :n2], 0.0).astype(BF16) for x in g]
    a_kk = [jnp.concatenate([jnp.where(strict, x[:n2, n2:], 0.0), jnp.where(incl, x[n2:, n2:], 0.0)],
                            axis=0).astype(BF16) for x in g]
    av = _each(_dot, a_kk, v_p)

    ld = [jnp.where(blk16, x, 0.0) for x in l_ab]
    lo = [x - y for x, y in zip(l_ab, ld)]
    l2 = _each(mm, ld, ld)
    l4 = _each(mm, l2, l2)
    l8 = _each(mm, l4, l4)
    dinv = _each(mm, [eye + x for x in ld], [eye + x for x in l2])
    dinv = _each(mm, dinv, [eye + x for x in l4])
    dinv = _each(mm, dinv, [eye + x for x in l8])
    m1 = _each(mm, dinv, lo)
    m2 = _each(mm, m1, m1)
    tinv = _each(mm, [eye + x for x in m1], [eye + x for x in m2])
    tinv = _each(mm, tinv, dinv)

    wu = _each(mm, tinv, [jnp.concatenate([a, x[:n2].astype(BF16)], axis=1) for a, x in zip(a_p, av)])
    wu_b = [x.astype(BF16) for x in wu]
    rb = _each(_dot, a_rb, wu_b)
    qw = [q + x[:, :n2] for q, x in zip(q_p, rb)]
    y0 = [x[:, n2:] + y[n2:] for x, y in zip(rb, av)]
    e = _each(_dot_tn, [x[:, :n2] for x in wu_b], [x[:n2] for x in btkt])
    f = _each(_dot_tn, [jnp.concatenate([x[:, n2:], y], axis=0) for x, y in zip(wu_b, v_p)], btkt)
    return [(a[:c] + a[c:], b[:c] + b[c:], ee, ff, p) for a, b, ee, ff, p in zip(qw, y0, e, f, pc)]


def _wkv_terms_kernel(r_ref, k_ref, v_ref, an_ref, bb_ref, lw_ref, qw_ref, y0_ref, e_ref, f_ref, pc_ref):
    c = WKV_CHUNK
    n2 = 2 * c
    row = lax.broadcasted_iota(jnp.int32, (n2, n2), 0)
    col = lax.broadcasted_iota(jnp.int32, (n2, n2), 1)
    tri_r = lax.broadcasted_iota(jnp.int32, (c, c), 0)
    tri_c = lax.broadcasted_iota(jnp.int32, (c, c), 1)
    consts = (
        jnp.where(tri_c <= tri_r, 1.0, 0.0).astype(BF16),
        lax.broadcasted_iota(jnp.int32, (c, WKV_PAIR), 1) < RWKV_HEAD_DIM,
        row > col,
        row >= col,
        (row >> 4) == (col >> 4),
        jnp.where(row == col, 1.0, 0.0),
    )
    n_chunks = r_ref.shape[0] // c
    chunks = []
    for j in range(n_chunks):
        rows = slice(j * c, (j + 1) * c)
        chunks.append(tuple(ref[rows, :].astype(F32) for ref in (r_ref, k_ref, v_ref, an_ref, bb_ref, lw_ref)))
    for j, (qw, y0, e, f, pc) in enumerate(_wkv_chunk_terms(chunks, consts)):
        rows = slice(j * c, (j + 1) * c)
        qw_ref[rows, :] = qw.astype(qw_ref.dtype)
        y0_ref[rows, :] = y0
        e_ref[j * n2:(j + 1) * n2, :] = e.astype(e_ref.dtype)
        f_ref[j * n2:(j + 1) * n2, :] = f
        pc_ref[j * 8:(j + 1) * 8, :] = jnp.broadcast_to(pc, (8, WKV_PAIR))


def _wkv_terms(r, k, v, an, bb, lw, batch, seq):
    n = batch * seq
    c = WKV_CHUNK
    n_pairs = MIX_WIDTH // WKV_PAIR
    cps = _pick_tile(seq // c, WKV_CHUNKS_PER_STEP, 1)
    steps = seq // (c * cps)
    tok = pl.BlockSpec((cps * c, WKV_PAIR), lambda b, p, i: (b * steps + i, p))
    mat = pl.BlockSpec((cps * WKV_PAIR, WKV_PAIR), lambda b, p, i: (b * steps + i, p))
    vec = pl.BlockSpec((cps * 8, WKV_PAIR), lambda b, p, i: (b * steps + i, p))
    n_chunks = n // c
    return pl.pallas_call(
        _wkv_terms_kernel,
        grid=(batch, n_pairs, steps),
        in_specs=[tok] * 6,
        out_specs=[tok, tok, mat, mat, vec],
        out_shape=[jax.ShapeDtypeStruct((n, MIX_WIDTH), BF16),
                   jax.ShapeDtypeStruct((n, MIX_WIDTH), F32),
                   jax.ShapeDtypeStruct((n_chunks * WKV_PAIR, MIX_WIDTH), BF16),
                   jax.ShapeDtypeStruct((n_chunks * WKV_PAIR, MIX_WIDTH), F32),
                   jax.ShapeDtypeStruct((n_chunks * 8, MIX_WIDTH), F32)],
        compiler_params=_params("parallel", "parallel", "parallel"),
        name="wkv_terms",
    )(r, k, v, an, bb, lw)


def _wkv_scan_kernel(qw_ref, y0_ref, e_ref, f_ref, pc_ref, bonus_ref, gate_ref, gng_ref, gnb_ref,
                     o_ref, state_ref):
    @pl.when(pl.program_id(2) == 0)
    def _():
        state_ref[...] = jnp.zeros_like(state_ref)

    n_pairs = state_ref.shape[1] // WKV_PAIR
    lane_lo = lax.broadcasted_iota(jnp.int32, (WKV_CHUNK, WKV_PAIR), 1) < RWKV_HEAD_DIM
    inv_n = 1.0 / RWKV_HEAD_DIM
    def head_mean(x):
        lo = jnp.sum(jnp.where(lane_lo, x, 0.0), axis=-1, keepdims=True)
        hi = jnp.sum(jnp.where(lane_lo, 0.0, x), axis=-1, keepdims=True)
        return jnp.where(lane_lo, lo, hi) * inv_n

    slices = [slice(p * WKV_PAIR, (p + 1) * WKV_PAIR) for p in range(n_pairs)]
    states = [state_ref[:, sl] for sl in slices]
    for j in range(qw_ref.shape[0] // WKV_CHUNK):
        rows = slice(j * WKV_CHUNK, (j + 1) * WKV_CHUNK)
        mrows = slice(j * WKV_PAIR, (j + 1) * WKV_PAIR)
        states_b = [s.astype(BF16) for s in states]
        ys = [_dot_nt(qw_ref[rows, sl], sb) for sl, sb in zip(slices, states_b)]
        se = [_dot(sb, e_ref[mrows, sl]) for sl, sb in zip(slices, states_b)]
        states = [s * pc_ref[8 * j:8 * j + 1, sl] + x + f_ref[mrows, sl] for sl, s, x in zip(slices, states, se)]
        for sl, y in zip(slices, ys):
            y = y + y0_ref[rows, sl]
            yc = y - head_mean(y)
            var = head_mean(yc * yc)
            yn = yc * lax.rsqrt(var + RWKV_GN_EPS) * gng_ref[:, sl] + gnb_ref[:, sl]
            o_ref[rows, sl] = ((yn + bonus_ref[rows, sl].astype(F32)) * gate_ref[rows, sl].astype(F32)
                               ).astype(o_ref.dtype)
    for sl, s in zip(slices, states):
        state_ref[:, sl] = s


def _wkv_scan(qw, y0, e, f, pc, bonus, gate, gn_g, gn_b, batch, seq):
    n = batch * seq
    cps = _pick_tile(seq // WKV_CHUNK, WKV_SCAN_CHUNKS_PER_STEP, 1)
    c = WKV_CHUNK * cps
    nc = seq // c
    width = MIX_WIDTH // 2
    groups = MIX_WIDTH // width
    tok = pl.BlockSpec((c, width), lambda b, g, i: (b * nc + i, g))
    mat = pl.BlockSpec((WKV_PAIR * cps, width), lambda b, g, i: (b * nc + i, g))
    vec = pl.BlockSpec((8 * cps, width), lambda b, g, i: (b * nc + i, g))
    par = pl.BlockSpec((1, width), lambda b, g, i: (0, g))
    return pl.pallas_call(
        _wkv_scan_kernel,
        grid=(batch, groups, nc),
        in_specs=[tok, tok, mat, mat, vec, tok, tok, par, par],
        out_specs=tok,
        out_shape=jax.ShapeDtypeStruct((n, MIX_WIDTH), BF16),
        scratch_shapes=[pltpu.VMEM((WKV_PAIR, width), F32)],
        compiler_params=_params("parallel", "parallel", "arbitrary"),
        name="wkv_scan",
    )(qw, y0, e, f, pc, bonus, gate, gn_g, gn_b)


def _route(x, w_hi, w_lo, b):
    x_hi, x_lo = _split2(x)
    logits = _dot(x_hi, w_hi) + (_dot(x_lo, w_hi) + _dot(x_hi, w_lo)) + b
    lane = lax.broadcasted_iota(jnp.int32, logits.shape, 1)
    neg_inf = -jnp.inf
    big = 4 * LANES

    def first_where(mask):
        return jnp.min(jnp.where(mask, lane, big), axis=-1, keepdims=True)

    gl = jnp.where(lane < N_GROUPS, logits, neg_inf)
    gmax = jnp.max(gl, axis=-1, keepdims=True)
    gidx = first_where(gl == gmax)
    p_group_sel = 1.0 / jnp.sum(jnp.exp(gl - gmax), axis=-1, keepdims=True)

    lo = N_GROUPS + EXPERTS_PER_GROUP * gidx
    emask = (lane >= lo) & (lane < lo + EXPERTS_PER_GROUP)
    el = jnp.where(emask, logits, neg_inf)
    emax = jnp.max(el, axis=-1, keepdims=True)
    pe = jnp.exp(el - emax)
    pe = pe / jnp.sum(pe, axis=-1, keepdims=True)
    pe = jnp.where(emask, pe, -1.0)
    p1 = jnp.max(pe, axis=-1, keepdims=True)
    i1 = first_where(pe == p1)
    pe2 = jnp.where(lane == i1, -1.0, pe)
    p2 = jnp.max(pe2, axis=-1, keepdims=True)
    i2 = first_where(pe2 == p2)
    denom = p1 + p2
    g1 = p_group_sel * p1 / denom
    g2 = p_group_sel * p2 / denom
    e1 = (i1 - N_GROUPS).astype(F32)
    e2 = (i2 - N_GROUPS).astype(F32)
    return jnp.where(lane == 0, e1, jnp.where(lane == 1, e2, jnp.where(lane == 2, g1, jnp.where(lane == 3, g2, 0.0))))


def _plan_kernel(e_ref, rank_ref, count_ref, carry_ref):
    @pl.when(pl.program_id(0) == 0)
    def _():
        carry_ref[...] = jnp.zeros_like(carry_ref)

    nb = e_ref.shape[-1]
    e = e_ref[0]
    onehot = jnp.where(lax.broadcasted_iota(jnp.int32, (N_EXPERTS, nb), 0) == e, 1.0, 0.0)
    r = lax.broadcasted_iota(jnp.int32, (nb, nb), 0)
    c = lax.broadcasted_iota(jnp.int32, (nb, nb), 1)
    before = jnp.where(r < c, 1.0, 0.0).astype(BF16)
    seen = _dot(onehot.astype(BF16), before) + carry_ref[:, 0:1]
    rank_ref[0] = jnp.sum(onehot * seen, axis=0, keepdims=True).astype(jnp.int32)
    carry_ref[...] = carry_ref[...] + jnp.sum(onehot, axis=1, keepdims=True)
    count_ref[...] = carry_ref[...]


def _dispatch_plan(e_flat):
    n_slots = e_flat.shape[0]
    nb = _pick_tile(n_slots, PLAN_BLOCK)
    steps = n_slots // nb
    rank, counts = pl.pallas_call(
        _plan_kernel,
        grid=(steps,),
        in_specs=[pl.BlockSpec((1, 1, nb), lambda i: (i, 0, 0))],
        out_specs=[pl.BlockSpec((1, 1, nb), lambda i: (i, 0, 0)),
                   pl.BlockSpec((N_EXPERTS, LANES), lambda i: (0, 0))],
        out_shape=[jax.ShapeDtypeStruct((steps, 1, nb), jnp.int32),
                   jax.ShapeDtypeStruct((N_EXPERTS, LANES), F32)],
        scratch_shapes=[pltpu.VMEM((N_EXPERTS, LANES), F32)],
        compiler_params=_params("arbitrary"),
        name="moe_plan",
    )(e_flat.reshape(steps, 1, nb))
    return rank.reshape(n_slots), counts[:, 0].astype(jnp.int32)


def _to_token_tiles(dst_ref, value):
    rows = value.shape[0]
    for j in range(TOKEN_TILE_ROWS):
        dst_ref[pl.ds(j, rows, stride=TOKEN_TILE_ROWS), :] = value[:, j * LANES:(j + 1) * LANES]


def _token_tile_columns(src_ref, rows, j):
    return src_ref[pl.ds(j, rows, stride=TOKEN_TILE_ROWS), :]


def _row_copy(table_hbm, dst_ref, src_row, dst_row, sem):
    src = pl.multiple_of(src_row * TOKEN_TILE_ROWS, TOKEN_TILE_ROWS)
    dst = pl.multiple_of(dst_row * TOKEN_TILE_ROWS, TOKEN_TILE_ROWS)
    return pltpu.make_async_copy(table_hbm.at[pl.ds(src, TOKEN_TILE_ROWS)],
                                 dst_ref.at[pl.ds(dst, TOKEN_TILE_ROWS)], sem)


def _token_gather(wait, idx_ref, offset, stride, n_tokens, table_hbm, dst_ref, sem):
    for i in range(n_tokens):
        if wait:
            _row_copy(table_hbm, dst_ref, 0, i, sem).wait()
        else:
            _row_copy(table_hbm, dst_ref, idx_ref[0, 0, offset + stride * i], i, sem).start(priority=i % 2)


def _expert_kernel(be_ref, nu_ref, idx_ref, idx_next_ref, x_hbm, wg_ref, wu_ref, wd_ref, o_ref,
                   wg_s, wu_s, wd_s, xb_s, xbuf, sem):
    i = pl.program_id(0)
    n_used = nu_ref[0]
    slot = i & 1

    def gather(wait, refs, slot):
        _token_gather(wait, refs, 0, 1, MOE_BLOCK, x_hbm, xbuf.at[slot], sem.at[slot])

    @pl.when(i == 0)
    def _():
        gather(False, idx_ref, 0)

    @pl.when(i + 1 < n_used)
    def _():
        gather(False, idx_next_ref, 1 - slot)

    prev = be_ref[jnp.maximum(i - 1, 0)]
    changed = (i == 0) | (be_ref[i] != prev)

    @pl.when(changed)
    def _():
        wg_s[...] = wg_ref[0, 0].astype(BF16)
        wu_s[...] = wu_ref[0, 0].astype(BF16)
        wd_s[...] = wd_ref[0, 0].astype(BF16)

    used = i < n_used

    @pl.when(used)
    def _():
        gather(True, idx_ref, slot)
        for j in range(TOKEN_TILE_ROWS):
            xb_s[:, j * LANES:(j + 1) * LANES] = _token_tile_columns(xbuf.at[slot], MOE_BLOCK, j).astype(BF16)
        parts = MOE_ROW_SPLIT
        rows = MOE_BLOCK // parts
        xs = [xb_s[p * rows:(p + 1) * rows, :] for p in range(parts)]
        g = [_dot(x, wg_s[...]) for x in xs]
        u = [_dot(x, wu_s[...]) for x in xs]
        h = [((a * jax.nn.sigmoid(a)) * b).astype(BF16) for a, b in zip(g, u)]
        y = [_dot(x, wd_s[...]) for x in h]
        for p in range(parts):
            _to_token_tiles(o_ref.at[pl.ds(p * rows * TOKEN_TILE_ROWS, rows * TOKEN_TILE_ROWS)], y[p])

    @pl.when(jnp.logical_not(used))
    def _():
        o_ref[...] = jnp.zeros_like(o_ref)


def _experts(x_tiles, src_tok, block_expert, n_used, w_gate, w_up, w_down, layer):
    cap = src_tok.shape[0]
    n_blk = cap // MOE_BLOCK
    wspec = lambda shape: pl.BlockSpec((1, 1) + shape, lambda i, be, nu: (layer, be[i], 0, 0))
    idx_spec = lambda f: pl.BlockSpec((1, 1, MOE_BLOCK), lambda i, be, nu: (f(i), 0, 0),
                                      memory_space=pltpu.SMEM)
    grid_spec = pltpu.PrefetchScalarGridSpec(
        num_scalar_prefetch=2,
        grid=(n_blk,),
        in_specs=[idx_spec(lambda i: i), idx_spec(lambda i: jnp.minimum(i + 1, n_blk - 1)),
                  pl.BlockSpec(memory_space=pl.ANY),
                  wspec((D_MODEL, D_EXPERT)), wspec((D_MODEL, D_EXPERT)), wspec((D_EXPERT, D_MODEL))],
        out_specs=pl.BlockSpec((MOE_BLOCK * TOKEN_TILE_ROWS, LANES), lambda i, be, nu: (i, 0)),
        scratch_shapes=[pltpu.VMEM((D_MODEL, D_EXPERT), BF16),
                        pltpu.VMEM((D_MODEL, D_EXPERT), BF16),
                        pltpu.VMEM((D_EXPERT, D_MODEL), BF16),
                        pltpu.VMEM((MOE_BLOCK, D_MODEL), BF16),
                        pltpu.VMEM((2, MOE_BLOCK * TOKEN_TILE_ROWS, LANES), F32),
                        pltpu.SemaphoreType.DMA((2,))],
    )
    idx = src_tok.reshape(n_blk, 1, MOE_BLOCK)
    return pl.pallas_call(
        _expert_kernel,
        grid_spec=grid_spec,
        out_shape=jax.ShapeDtypeStruct((cap * TOKEN_TILE_ROWS, LANES), F32),
        compiler_params=_params("arbitrary"),
        name="moe_experts",
    )(block_expert, n_used, idx, idx, x_tiles, w_gate, w_up, w_down)


def _combine_ln_kernel(idx_ref, idx_next_ref, ys_hbm, slab_ref, x_ref, g_ref, b_ref, xo_ref, xb_ref,
                       buf, y_s, sem):
    i = pl.program_id(0)
    slot = i & 1
    tm = x_ref.shape[0]

    def gather(wait, refs, slot):
        for k in range(TOP_K):
            _token_gather(wait, refs, k, TOP_K, tm, ys_hbm, buf.at[slot, k], sem.at[slot, k])

    @pl.when(i == 0)
    def _():
        gather(False, idx_ref, 0)

    @pl.when(i + 1 < pl.num_programs(0))
    def _():
        gather(False, idx_next_ref, 1 - slot)

    slab = slab_ref[...]
    lane = lax.broadcasted_iota(jnp.int32, slab.shape, 1)
    g1 = jnp.sum(jnp.where(lane == 2, slab, 0.0), axis=-1, keepdims=True)
    g2 = jnp.sum(jnp.where(lane == 3, slab, 0.0), axis=-1, keepdims=True)
    gather(True, idx_ref, slot)
    for j in range(TOKEN_TILE_ROWS):
        cols = slice(j * LANES, (j + 1) * LANES)
        f = (_token_tile_columns(buf.at[slot, 0], tm, j) * g1
             + _token_tile_columns(buf.at[slot, 1], tm, j) * g2)
        y_s[:, cols] = DEEPNORM_ALPHA * x_ref[:, cols] + f
    out = _layer_norm_rows(y_s[...], g_ref[...], b_ref[...])
    xo_ref[...] = out
    xb_ref[...] = out.astype(BF16)


def _combine_ln(ys, dest, slab, x, g, b):
    n = x.shape[0]
    tm = _pick_tile(n, 256, 8)
    steps = n // tm
    row = lambda i: (i, 0)
    fixed = lambda i: (0, 0)
    idx_spec = lambda f: pl.BlockSpec((1, 1, TOP_K * tm), lambda i: (f(i), 0, 0), memory_space=pltpu.SMEM)
    idx = dest.reshape(steps, 1, TOP_K * tm)
    return pl.pallas_call(
        _combine_ln_kernel,
        grid=(steps,),
        in_specs=[idx_spec(lambda i: i), idx_spec(lambda i: jnp.minimum(i + 1, steps - 1)),
                  pl.BlockSpec(memory_space=pl.ANY),
                  pl.BlockSpec((tm, LANES), row),
                  pl.BlockSpec((tm, D_MODEL), row),
                  pl.BlockSpec((1, D_MODEL), fixed),
                  pl.BlockSpec((1, D_MODEL), fixed)],
        out_specs=[pl.BlockSpec((tm, D_MODEL), row), pl.BlockSpec((tm, D_MODEL), row)],
        out_shape=[jax.ShapeDtypeStruct((n, D_MODEL), F32), jax.ShapeDtypeStruct((n, D_MODEL), BF16)],
        scratch_shapes=[pltpu.VMEM((2, TOP_K, tm * TOKEN_TILE_ROWS, LANES), F32),
                        pltpu.VMEM((tm, D_MODEL), F32),
                        pltpu.SemaphoreType.DMA((2, TOP_K))],
        compiler_params=_params("arbitrary"),
        name="moe_combine_ln",
    )(idx, idx, ys, slab, x, g, b)


def _router_params(rg_w, rg_b, re_w, re_b):
    pad_cols = LANES - N_GROUPS - N_EXPERTS
    w_router = jnp.concatenate([rg_w, re_w, jnp.zeros((D_MODEL, pad_cols), F32)], axis=1)
    b_router = jnp.concatenate([rg_b, re_b, jnp.zeros((pad_cols,), F32)])[None, :]
    w_hi = w_router.astype(BF16)
    w_lo = (w_router - w_hi.astype(F32)).astype(BF16)
    return jnp.stack([w_hi, w_lo]), b_router


def _moe_ln(x, x_tiles, slab, w_gate, w_up, w_down, layer, ln_g, ln_b):
    n = x.shape[0]
    n_slots = n * TOP_K

    e_flat = slab[:, :TOP_K].astype(jnp.int32).reshape(n_slots)
    rank, counts = _dispatch_plan(e_flat)
    padded = (counts + MOE_BLOCK - 1) // MOE_BLOCK * MOE_BLOCK
    pad_end = jnp.cumsum(padded)
    pad_start = pad_end - padded
    dest = pad_start[e_flat] + rank
    cap = n_slots + N_EXPERTS * MOE_BLOCK
    n_blk = cap // MOE_BLOCK
    src_tok = jnp.zeros((cap,), jnp.int32).at[dest].set(jnp.arange(n_slots, dtype=jnp.int32) // TOP_K)
    block_start = jnp.arange(n_blk, dtype=jnp.int32) * MOE_BLOCK
    block_expert = jnp.minimum(jnp.sum((pad_end[None, :] <= block_start[:, None]).astype(jnp.int32), axis=1),
                               N_EXPERTS - 1)
    n_used = (pad_end[-1:] // MOE_BLOCK).astype(jnp.int32)

    ys = _experts(x_tiles, src_tok, block_expert, n_used, w_gate, w_up, w_down, layer)
    return _combine_ln(ys, dest, slab, x, ln_g[None, :], ln_b[None, :])


def _rwkv_mixer(x_bf, w_in, mu, w0, w2, a0, a2, g2, k_k, k_a, r_k, gn_g, gn_b, batch, seq):
    c3 = 3 * MIX_WIDTH
    lo_w, lo_a, lo_g = c3, c3 + DECAY_LORA, c3 + DECAY_LORA + AAA_LORA
    pad = LANES - DECAY_LORA

    def permute_cols(t, with_mem):
        zeros = jnp.zeros(t.shape[:-1] + (pad,), t.dtype)
        parts = [t[..., :c3]]
        if with_mem:
            parts.append(t[..., RWKV_COLS:])
        parts += [t[..., lo_w:lo_a], zeros, t[..., lo_a:lo_g], zeros, t[..., lo_g:RWKV_COLS]]
        return jnp.concatenate(parts, axis=-1)

    w_perm = permute_cols(w_in, True).astype(BF16)
    mu_perm = permute_cols(mu[None, :], False)
    mu_main, mu_lora = mu_perm[:, :c3], mu_perm[:, c3:]
    z = _matmul(x_bf, w_perm, BF16, "rwkv_in_proj")

    zrow = jnp.zeros((pad, MIX_WIDTH), F32)
    w2p = jnp.concatenate([w2, zrow], axis=0).astype(BF16)
    a2p = jnp.concatenate([a2, zrow], axis=0).astype(BF16)
    r, k, v, an, bb, lw, gate, bonus = _rwkv_prep(
        z, mu_main, mu_lora, w2p, a2p, g2.astype(BF16), w0[None, :], a0[None, :], k_k[None, :], k_a[None, :],
        r_k.reshape(1, MIX_WIDTH), batch, seq)
    qw, y0, e, f, pc = _wkv_terms(r, k, v, an, bb, lw, batch, seq)
    mix = _wkv_scan(qw, y0, e, f, pc, bonus, gate, gn_g[None, :], gn_b[None, :], batch, seq)
    return z, mix, c3 // COL_BLOCK


def kernel(x, mem, w_out, mem_wk, mem_wv, ln_mix_g, ln_mix_b, ln_ffn_g, ln_ffn_b, router_group_w, router_group_b, router_expert_w, router_expert_b, expert_w_gate, expert_w_up, expert_w_down, rwkv_w_in, rwkv_mu, rwkv_w0, rwkv_w2, rwkv_a0, rwkv_a2, rwkv_g2, rwkv_k_k, rwkv_k_a, rwkv_r_k, rwkv_gn_g, rwkv_gn_b, conv_w_in, conv_w, sb_w_in):
    batch, seq, d = x.shape
    assert d == D_MODEL and mem.shape == (batch, MEM_LEN, D_MODEL)
    n = batch * seq
    xf = x.reshape(n, D_MODEL)
    xb = xf.astype(BF16)
    mem_bf = mem.reshape(batch * MEM_LEN, D_MODEL).astype(BF16)
    q_block = 3 * MIX_WIDTH // COL_BLOCK

    for i in range(DEPTH):
        kind, j = i % N_MIXERS, i // N_MIXERS
        if kind == 0:
            z, mix, _ = _rwkv_mixer(xb, rwkv_w_in[j], rwkv_mu[j], rwkv_w0[j], rwkv_w2[j], rwkv_a0[j], rwkv_a2[j],
                                    rwkv_g2[j], rwkv_k_k[j], rwkv_k_a[j], rwkv_r_k[j], rwkv_gn_g[j], rwkv_gn_b[j],
                                    batch, seq)
        elif kind == 1:
            z = _matmul(xb, conv_w_in[j].astype(BF16), BF16, "conv_in_proj")
            mix = _short_conv(z, conv_w[j], batch, seq)
        else:
            z = _matmul(xb, sb_w_in[j].astype(BF16), BF16, "sb_in_proj")
            mix = _stick_breaking(z, batch, seq)
        w_kv = jnp.concatenate([mem_wk[i], mem_wv[i]], axis=1).astype(BF16)
        kv = _matmul(mem_bf, w_kv, BF16, "mem_kv_proj")
        mem_out = _mem_attention(z, kv, batch, seq, q_block)
        w_o = w_out[i].astype(BF16)
        w_router, b_router = _router_params(router_group_w[i], router_group_b[i],
                                            router_expert_w[i], router_expert_b[i])
        xf, x_tiles, slab = _outproj_ln(mix, mem_out, w_o[:MIX_WIDTH], w_o[MIX_WIDTH:], xf,
                                        ln_mix_g[i][None, :], ln_mix_b[i][None, :], w_router, b_router)
        xf, xb = _moe_ln(xf, x_tiles, slab, expert_w_gate, expert_w_up, expert_w_down, i,
                         ln_ffn_g[i], ln_ffn_b[i])
    return xf.reshape(batch, seq, D_MODEL)
```

```python
import functools
import math

import jax
import jax.numpy as jnp
from jax import lax
from jax.experimental import pallas as pl
from jax.experimental.pallas import tpu as pltpu

F32 = jnp.float32
BF16 = jnp.bfloat16

D_MODEL = 2048
DEPTH = 4
N_MIXERS = 3
MEM_LEN = 256
MIX_WIDTH = 1536
MEM_HEADS = 4
MEM_HEAD_DIM = 128
MEM_WIDTH = 512
RWKV_HEAD_DIM = 64
RWKV_HEADS = 24
DECAY_LORA = 96
AAA_LORA = 96
GATE_LORA = 256
RWKV_GN_EPS = 64e-5
RWKV_COLS = 3 * MIX_WIDTH + DECAY_LORA + AAA_LORA + GATE_LORA
SB_HEAD_DIM = 128
SB_HEADS = 12
N_GROUPS = 4
EXPERTS_PER_GROUP = 8
N_EXPERTS = 32
TOP_K = 2
D_EXPERT = 512
DEEPNORM_ALPHA = (2 * DEPTH) ** 0.25
LN_EPS = 1e-5

LANES = 128
V7X_VMEM_LIMIT_BYTES = 56 << 20

COL_BLOCK = 512
MOE_BLOCK = 256
MOE_ROW_SPLIT = 2
PLAN_BLOCK = 512
GATHER_GROUP = 8
TOKEN_TILE_ROWS = D_MODEL // LANES
WKV_CHUNK = 64
WKV_PAIR = 2 * RWKV_HEAD_DIM
WKV_CHUNKS_PER_STEP = 8
WKV_SCAN_CHUNKS_PER_STEP = 4
SB_Q_BLOCK = 256
SB_GROUP = 4


def _params(*sem):
    return pltpu.CompilerParams(dimension_semantics=sem, vmem_limit_bytes=V7X_VMEM_LIMIT_BYTES)


def _dot(a, b):
    return jnp.dot(a, b, preferred_element_type=F32)


def _dot_nt(a, b):
    return lax.dot_general(a, b, (((1,), (1,)), ((), ())), preferred_element_type=F32)


def _dot_tn(a, b):
    return lax.dot_general(a, b, (((0,), (0,)), ((), ())), preferred_element_type=F32)


def _split2(x):
    hi = x.astype(BF16)
    lo = (x - hi.astype(F32)).astype(BF16)
    return hi, lo


def _dot_hl(x, w):
    hi, lo = _split2(x)
    return _dot(hi, w) + _dot(lo, w)


def _pick_tile(n, target, quantum=LANES):
    best = None
    for t in range(quantum, min(n, target) + 1, quantum):
        if n % t == 0:
            best = t
    assert best is not None, (n, target)
    return best


def _layer_norm_rows(y, g, b):
    mu = jnp.mean(y, axis=-1, keepdims=True)
    yc = y - mu
    var = jnp.mean(yc * yc, axis=-1, keepdims=True)
    return yc * lax.rsqrt(var + LN_EPS) * g + b


def _matmul_kernel(x_ref, w_ref, o_ref):
    o_ref[...] = _dot(x_ref[...], w_ref[...]).astype(o_ref.dtype)


def _matmul(x, w, out_dtype, name):
    m, k = x.shape
    n = w.shape[1]
    tm = _pick_tile(m, 1024, 8)
    tn = _pick_tile(n, 1536)
    return pl.pallas_call(
        _matmul_kernel,
        grid=(n // tn, m // tm),
        in_specs=[pl.BlockSpec((tm, k), lambda j, i: (i, 0)),
                  pl.BlockSpec((k, tn), lambda j, i: (0, j))],
        out_specs=pl.BlockSpec((tm, tn), lambda j, i: (i, j)),
        out_shape=jax.ShapeDtypeStruct((m, n), out_dtype),
        compiler_params=_params("parallel", "arbitrary"),
        name=name,
    )(x, w)


def _mem_attn_kernel(q_ref, kv_ref, o_ref):
    scale = 1.0 / math.sqrt(MEM_HEAD_DIM)
    for h in range(MEM_HEADS):
        sl = slice(h * MEM_HEAD_DIM, (h + 1) * MEM_HEAD_DIM)
        q = q_ref[:, sl]
        km = kv_ref[:, sl]
        vm = kv_ref[:, MEM_WIDTH + h * MEM_HEAD_DIM:MEM_WIDTH + (h + 1) * MEM_HEAD_DIM]
        s = _dot_nt(q, km) * scale
        s = s - jnp.max(s, axis=-1, keepdims=True)
        p = jnp.exp(s)
        p = p / jnp.sum(p, axis=-1, keepdims=True)
        o_ref[:, sl] = _dot(p.astype(BF16), vm).astype(o_ref.dtype)


def _mem_attention(z, kv, batch, seq, q_col_block):
    tq = _pick_tile(seq, 512, 8)
    nt = seq // tq
    return pl.pallas_call(
        _mem_attn_kernel,
        grid=(batch, nt),
        in_specs=[pl.BlockSpec((tq, MEM_WIDTH), lambda b, i: (b * nt + i, q_col_block)),
                  pl.BlockSpec((MEM_LEN, 2 * MEM_WIDTH), lambda b, i: (b, 0))],
        out_specs=pl.BlockSpec((tq, MEM_WIDTH), lambda b, i: (b * nt + i, 0)),
        out_shape=jax.ShapeDtypeStruct((batch * seq, MEM_WIDTH), BF16),
        compiler_params=_params("parallel", "parallel"),
        name="mem_attention",
    )(z, kv)


def _outproj_ln_kernel(mix_ref, mem_ref, w1_ref, w2_ref, x_ref, g_ref, b_ref, wr_ref, br_ref,
                       xo_ref, xt_ref, slab_ref):
    h = _dot(mix_ref[...], w1_ref[...]) + _dot(mem_ref[...], w2_ref[...])
    y = DEEPNORM_ALPHA * x_ref[...] + h
    out = _layer_norm_rows(y, g_ref[...], b_ref[...])
    xo_ref[...] = out
    _to_token_tiles(xt_ref, out)
    slab_ref[...] = _route(out, wr_ref[0], wr_ref[1], br_ref[...])


def _outproj_ln(mix, mem_out, w_mix, w_mem, x, g, b, w_router, b_router):
    n = x.shape[0]
    tm = _pick_tile(n, 512, 8)
    row = lambda i: (i, 0)
    fixed = lambda i: (0, 0)
    return pl.pallas_call(
        _outproj_ln_kernel,
        grid=(n // tm,),
        in_specs=[pl.BlockSpec((tm, MIX_WIDTH), row),
                  pl.BlockSpec((tm, MEM_WIDTH), row),
                  pl.BlockSpec((MIX_WIDTH, D_MODEL), fixed),
                  pl.BlockSpec((MEM_WIDTH, D_MODEL), fixed),
                  pl.BlockSpec((tm, D_MODEL), row),
                  pl.BlockSpec((1, D_MODEL), fixed),
                  pl.BlockSpec((1, D_MODEL), fixed),
                  pl.BlockSpec((2, D_MODEL, LANES), lambda i: (0, 0, 0)),
                  pl.BlockSpec((1, LANES), fixed)],
        out_specs=[pl.BlockSpec((tm, D_MODEL), row), pl.BlockSpec((tm * TOKEN_TILE_ROWS, LANES), row),
                   pl.BlockSpec((tm, LANES), row)],
        out_shape=[jax.ShapeDtypeStruct((n, D_MODEL), F32),
                   jax.ShapeDtypeStruct((n * TOKEN_TILE_ROWS, LANES), F32),
                   jax.ShapeDtypeStruct((n, LANES), F32)],
        compiler_params=_params("parallel"),
        name="outproj_ln",
    )(mix, mem_out, w_mix, w_mem, x, g, b, w_router, b_router)


def _conv_kernel(b_ref, c_ref, h_ref, w_ref, o_ref, carry_ref):
    @pl.when(pl.program_id(1) == 0)
    def _():
        carry_ref[...] = jnp.zeros_like(carry_ref)

    u = c_ref[...].astype(F32) * h_ref[...].astype(F32)
    tt = u.shape[0]
    rowid = lax.broadcasted_iota(jnp.int32, u.shape, 0)
    c0 = carry_ref[0:1, :]
    c1 = carry_ref[1:2, :]
    u1 = jnp.where(rowid == 0, c1, pltpu.roll(u, 1, axis=0))
    u2 = jnp.where(rowid == 0, c0, jnp.where(rowid == 1, c1, pltpu.roll(u, 2, axis=0)))
    carry_ref[0:2, :] = u[tt - 2:tt, :]
    y = w_ref[0:1, :] * u2 + w_ref[1:2, :] * u1 + w_ref[2:3, :] * u
    o_ref[...] = (b_ref[...].astype(F32) * y).astype(o_ref.dtype)


def _short_conv(z, conv_w, batch, seq):
    tt = _pick_tile(seq, 512, 8)
    nt = seq // tt
    spec = lambda c: pl.BlockSpec((tt, MIX_WIDTH), lambda b, i: (b * nt + i, c))
    return pl.pallas_call(
        _conv_kernel,
        grid=(batch, nt),
        in_specs=[spec(0), spec(1), spec(2),
                  pl.BlockSpec((3, MIX_WIDTH), lambda b, i: (0, 0))],
        out_specs=pl.BlockSpec((tt, MIX_WIDTH), lambda b, i: (b * nt + i, 0)),
        out_shape=jax.ShapeDtypeStruct((batch * seq, MIX_WIDTH), BF16),
        scratch_shapes=[pltpu.VMEM((8, MIX_WIDTH), F32)],
        compiler_params=_params("parallel", "arbitrary"),
        name="short_conv",
    )(z, z, z, conv_w)


def _sb_kernel(q_ref, k_ref, v_ref, suffix_ref, o_ref):
    i = pl.program_id(2)
    tq = q_ref.shape[0]
    bk = tq
    scale = math.log2(math.e) / math.sqrt(SB_HEAD_DIM)
    q = q_ref[...]
    suffix = suffix_ref[...]

    def rows_of(ref, j):
        return ref[pl.ds(pl.multiple_of(j * bk, bk), bk), :]

    def scores(j):
        return _dot_nt(q, rows_of(k_ref, j)) * scale

    def logs(s, valid):
        soft = jnp.log2(1.0 + jnp.exp2(-jnp.abs(s)))
        log_beta = jnp.minimum(s, 0.0) - soft
        log_not = log_beta - s
        if valid is not None:
            log_not = jnp.where(valid, log_not, 0.0)
        hi, lo = _split2(log_not)
        return log_beta, jnp.concatenate([hi, lo], axis=1), jnp.sum(log_not, axis=-1, keepdims=True)

    def weights(log_beta, within, later, valid):
        attn = jnp.exp2(log_beta + within + later)
        if valid is not None:
            attn = jnp.where(valid, attn, 0.0)
        return attn.astype(BF16)

    def blocks(js, later, valid):
        s = [scores(j) for j in js]
        lg = [logs(x, valid) for x in s]
        within = [_dot(x[1], suffix) for x in lg]
        pv = None
        for j, (log_beta, _, total), w in zip(js, lg, within):
            p = _dot(weights(log_beta, w, later, valid), rows_of(v_ref, j))
            pv = p if pv is None else pv + p
            later = later + total
        return pv, later

    valid = lax.broadcasted_iota(jnp.int32, (tq, bk), 1) < lax.broadcasted_iota(jnp.int32, (tq, bk), 0)
    acc, later = blocks([i], jnp.zeros((tq, 1), F32), valid)

    def group(width, first):
        def body(step, carry):
            acc, later = carry
            j = first - width * step
            pv, later = blocks([j - d for d in range(width)], later, None)
            return acc + pv, later
        return body

    n1 = i & 1
    n2 = lax.shift_right_logical(i, 1) & 1
    carry = lax.fori_loop(0, n1, group(1, i - 1), (acc, later))
    carry = lax.fori_loop(0, n2, group(2, i - 1 - n1), carry)
    acc, _ = lax.fori_loop(0, lax.shift_right_logical(i, 2), group(SB_GROUP, i - 1 - n1 - 2 * n2), carry)
    o_ref[...] = acc.astype(o_ref.dtype)


def _stick_breaking(z, batch, seq):
    tq = _pick_tile(seq, SB_Q_BLOCK, LANES)
    nq = seq // tq
    later_key = jnp.arange(tq)[:, None] > jnp.arange(tq)[None, :]
    suffix = jnp.concatenate([later_key, later_key], axis=0).astype(BF16)
    return pl.pallas_call(
        _sb_kernel,
        grid=(batch, SB_HEADS, nq),
        in_specs=[pl.BlockSpec((tq, SB_HEAD_DIM), lambda b, h, i: (b * nq + i, h)),
                  pl.BlockSpec((seq, SB_HEAD_DIM), lambda b, h, i: (b, SB_HEADS + h)),
                  pl.BlockSpec((seq, SB_HEAD_DIM), lambda b, h, i: (b, 2 * SB_HEADS + h)),
                  pl.BlockSpec((2 * tq, tq), lambda b, h, i: (0, 0))],
        out_specs=pl.BlockSpec((tq, SB_HEAD_DIM), lambda b, h, i: (b * nq + i, h)),
        out_shape=jax.ShapeDtypeStruct((batch * seq, MIX_WIDTH), BF16),
        compiler_params=_params("parallel", "parallel", "arbitrary"),
        name="stick_breaking",
    )(z, z, z, suffix)


def _rwkv_prep_kernel(zr_ref, zk_ref, zv_ref, zl_ref, mur_ref, muk_ref, muv_ref, mul_ref,
                      w2_ref, a2_ref, g2_ref, w0_ref, a0_ref, kk_ref, ka_ref, rk_ref,
                      seg_ref, segt_ref,
                      r_out, k_out, v_out, an_out, bb_out, lw_out, gate_out, bonus_out,
                      carry_ref):
    @pl.when(pl.program_id(2) == 0)
    def _():
        carry_ref[...] = jnp.zeros_like(carry_ref)

    def shifted(z_ref, slot, mu_ref):
        z = z_ref[...].astype(F32)
        tt = z.shape[0]
        rowid = lax.broadcasted_iota(jnp.int32, z.shape, 0)
        prev = jnp.where(rowid == 0, carry_ref[slot:slot + 1, :], pltpu.roll(z, 1, axis=0))
        carry_ref[slot:slot + 1, :] = z[tt - 1:tt, :]
        return z + (prev - z) * mu_ref[...]

    r = shifted(zr_ref, 0, mur_ref)
    k = shifted(zk_ref, 1, muk_ref)
    v = shifted(zv_ref, 2, muv_ref)
    zl = shifted(zl_ref, 3, mul_ref)
    zw = zl[:, 0:LANES]
    za = zl[:, LANES:2 * LANES]
    zg = zl[:, 2 * LANES:4 * LANES]

    dw = w0_ref[...] + _dot(jnp.tanh(zw).astype(BF16), w2_ref[...])
    neg = -dw
    softplus = jnp.maximum(neg, 0.0) + jnp.log(1.0 + jnp.exp(-jnp.abs(neg)))
    lw = -jnp.exp(-softplus - 0.5)
    a = jax.nn.sigmoid(a0_ref[...] + _dot(za.astype(BF16), a2_ref[...]))
    gate = _dot(jax.nn.sigmoid(zg).astype(BF16), g2_ref[...])

    seg = seg_ref[...]
    segt = segt_ref[...]
    kk = k * kk_ref[...]
    norm = jnp.sqrt(_dot_hl(_dot_hl(kk * kk, seg), segt))
    kkn = kk / jnp.maximum(norm, 1e-12)
    k2 = k * (1.0 + (a - 1.0) * ka_ref[...])
    bonus = _dot_hl(_dot_hl(r * k2 * rk_ref[...], seg), segt) * v

    r_out[...] = r.astype(r_out.dtype)
    k_out[...] = k2.astype(k_out.dtype)
    v_out[...] = v.astype(v_out.dtype)
    an_out[...] = (-kkn).astype(an_out.dtype)
    bb_out[...] = (kkn * a).astype(bb_out.dtype)
    lw_out[...] = lw
    gate_out[...] = gate.astype(gate_out.dtype)
    bonus_out[...] = bonus.astype(bonus_out.dtype)


def _rwkv_prep(z, mu_main, mu_lora, w2p, a2p, g2, w0, a0, k_k, k_a, r_k, batch, seq):
    n = batch * seq
    tt = _pick_tile(seq, 256, 8)
    nt = seq // tt
    nc = MIX_WIDTH // COL_BLOCK
    heads_per_block = COL_BLOCK // RWKV_HEAD_DIM
    ch = jnp.arange(COL_BLOCK)[:, None] // RWKV_HEAD_DIM
    seg = (ch == jnp.arange(LANES)[None, :]).astype(BF16)
    segt = seg.T
    assert heads_per_block <= LANES

    zspec = lambda off: pl.BlockSpec((tt, COL_BLOCK), lambda c, b, i: (b * nt + i, off + c))
    zlspec = pl.BlockSpec((tt, COL_BLOCK), lambda c, b, i: (b * nt + i, 3 * nc + 1))
    muspec = lambda off: pl.BlockSpec((1, COL_BLOCK), lambda c, b, i: (0, off + c))
    colspec = lambda rows: pl.BlockSpec((rows, COL_BLOCK), lambda c, b, i: (0, c))
    fixed = lambda shape: pl.BlockSpec(shape, lambda c, b, i: (0, 0))
    ospec = pl.BlockSpec((tt, COL_BLOCK), lambda c, b, i: (b * nt + i, c))
    outs = pl.pallas_call(
        _rwkv_prep_kernel,
        grid=(nc, batch, nt),
        in_specs=[zspec(0), zspec(nc), zspec(2 * nc), zlspec,
                  muspec(0), muspec(nc), muspec(2 * nc), fixed((1, COL_BLOCK)),
                  colspec(LANES), colspec(LANES), colspec(GATE_LORA),
                  colspec(1), colspec(1), colspec(1), colspec(1), colspec(1),
                  fixed((COL_BLOCK, LANES)), fixed((LANES, COL_BLOCK))],
        out_specs=[ospec] * 8,
        out_shape=[jax.ShapeDtypeStruct((n, MIX_WIDTH), F32 if name == "lw" else BF16)
                   for name in ("r", "k", "v", "an", "bb", "lw", "gate", "bonus")],
        scratch_shapes=[pltpu.VMEM((8, COL_BLOCK), F32)],
        compiler_params=_params("parallel", "parallel", "arbitrary"),
        name="rwkv_prep",
    )(z, z, z, z, mu_main, mu_main, mu_main, mu_lora, w2p, a2p, g2, w0, a0, k_k, k_a, r_k, seg, segt)
    return outs


def _each(fn, *cols):
    return [fn(*args) for args in zip(*cols)]


def _wkv_chunk_terms(chunks, consts):
    tri, lane_lo, strict, incl, blk16, eye = consts
    c = WKV_CHUNK
    n2 = 2 * c

    def pair_rows(x):
        return jnp.concatenate([jnp.where(lane_lo, x, 0.0), jnp.where(lane_lo, 0.0, x)], axis=0)

    def mm(x, y):
        return _dot(x.astype(BF16), y.astype(BF16))

    def cumulative(r, k, v, an, bb, lw):
        h1 = lw.astype(BF16)
        r1 = lw - h1.astype(F32)
        h2 = r1.astype(BF16)
        h3 = (r1 - h2.astype(F32)).astype(BF16)
        return _dot(tri, h1) + _dot(tri, h2) + _dot(tri, h3)

    cums = _each(cumulative, *zip(*chunks))

    def operands(chunk, cum):
        r, k, v, an, bb, lw = chunk
        cum_last = cum[c - 1:c, :]
        p_inv = jnp.exp(-cum)
        p_tail = jnp.exp(cum_last - cum)
        a_p = pair_rows(an * jnp.exp(cum - lw)).astype(BF16)
        q_p = pair_rows(r * jnp.exp(cum))
        b_p = pair_rows(bb * p_inv).astype(BF16)
        k_p = pair_rows(k * p_inv).astype(BF16)
        v_p = pair_rows(v).astype(BF16)
        bt_p = pair_rows(bb * p_tail).astype(BF16)
        kt_p = pair_rows(k * p_tail).astype(BF16)
        aq = jnp.concatenate([a_p, q_p.astype(BF16)], axis=0)
        bk = jnp.concatenate([b_p, k_p], axis=0)
        return a_p, q_p, v_p, jnp.concatenate([bt_p, kt_p], axis=0), aq, bk, jnp.exp(cum_last)

    a_p, q_p, v_p, btkt, aq, bk, pc = zip(*_each(operands, chunks, cums))
    g = _each(_dot_nt, aq, bk)
    l_ab = [jnp.where(strict, x[:n2, :n2], 0.0) for x in g]
    a_rb = [jnp.where(incl, x[n2:, :n2], 0.0).astype(BF16) for x in g]
    a_kk = [jnp.concatenate([jnp.where(strict, x[:n2, n2:], 0.0), jnp.where(incl, x[n2:, n2:], 0.0)],
                            axis=0).astype(BF16) for x in g]
    av = _each(_dot, a_kk, v_p)

    ld = [jnp.where(blk16, x, 0.0) for x in l_ab]
    lo = [x - y for x, y in zip(l_ab, ld)]
    l2 = _each(mm, ld, ld)
    l4 = _each(mm, l2, l2)
    l8 = _each(mm, l4, l4)
    dinv = _each(mm, [eye + x for x in ld], [eye + x for x in l2])
    dinv = _each(mm, dinv, [eye + x for x in l4])
    dinv = _each(mm, dinv, [eye + x for x in l8])
    m1 = _each(mm, dinv, lo)
    m2 = _each(mm, m1, m1)
    tinv = _each(mm, [eye + x for x in m1], [eye + x for x in m2])
    tinv = _each(mm, tinv, dinv)

    wu = _each(mm, tinv, [jnp.concatenate([a, x[:n2].astype(BF16)], axis=1) for a, x in zip(a_p, av)])
    wu_b = [x.astype(BF16) for x in wu]
    rb = _each(_dot, a_rb, wu_b)
    qw = [q + x[:, :n2] for q, x in zip(q_p, rb)]
    y0 = [x[:, n2:] + y[n2:] for x, y in zip(rb, av)]
    e = _each(_dot_tn, [x[:, :n2] for x in wu_b], [x[:n2] for x in btkt])
    f = _each(_dot_tn, [jnp.concatenate([x[:, n2:], y], axis=0) for x, y in zip(wu_b, v_p)], btkt)
    return [(a[:c] + a[c:], b[:c] + b[c:], ee, ff, p) for a, b, ee, ff, p in zip(qw, y0, e, f, pc)]


def _wkv_terms_kernel(r_ref, k_ref, v_ref, an_ref, bb_ref, lw_ref, qw_ref, y0_ref, e_ref, f_ref, pc_ref):
    c = WKV_CHUNK
    n2 = 2 * c
    row = lax.broadcasted_iota(jnp.int32, (n2, n2), 0)
    col = lax.broadcasted_iota(jnp.int32, (n2, n2), 1)
    tri_r = lax.broadcasted_iota(jnp.int32, (c, c), 0)
    tri_c = lax.broadcasted_iota(jnp.int32, (c, c), 1)
    consts = (
        jnp.where(tri_c <= tri_r, 1.0, 0.0).astype(BF16),
        lax.broadcasted_iota(jnp.int32, (c, WKV_PAIR), 1) < RWKV_HEAD_DIM,
        row > col,
        row >= col,
        (row >> 4) == (col >> 4),
        jnp.where(row == col, 1.0, 0.0),
    )
    n_chunks = r_ref.shape[0] // c
    chunks = []
    for j in range(n_chunks):
        rows = slice(j * c, (j + 1) * c)
        chunks.append(tuple(ref[rows, :].astype(F32) for ref in (r_ref, k_ref, v_ref, an_ref, bb_ref, lw_ref)))
    for j, (qw, y0, e, f, pc) in enumerate(_wkv_chunk_terms(chunks, consts)):
        rows = slice(j * c, (j + 1) * c)
        qw_ref[rows, :] = qw.astype(qw_ref.dtype)
        y0_ref[rows, :] = y0
        e_ref[j * n2:(j + 1) * n2, :] = e.astype(e_ref.dtype)
        f_ref[j * n2:(j + 1) * n2, :] = f
        pc_ref[j * 8:(j + 1) * 8, :] = jnp.broadcast_to(pc, (8, WKV_PAIR))


def _wkv_terms(r, k, v, an, bb, lw, batch, seq):
    n = batch * seq
    c = WKV_CHUNK
    n_pairs = MIX_WIDTH // WKV_PAIR
    cps = _pick_tile(seq // c, WKV_CHUNKS_PER_STEP, 1)
    steps = seq // (c * cps)
    tok = pl.BlockSpec((cps * c, WKV_PAIR), lambda b, p, i: (b * steps + i, p))
    mat = pl.BlockSpec((cps * WKV_PAIR, WKV_PAIR), lambda b, p, i: (b * steps + i, p))
    vec = pl.BlockSpec((cps * 8, WKV_PAIR), lambda b, p, i: (b * steps + i, p))
    n_chunks = n // c
    return pl.pallas_call(
        _wkv_terms_kernel,
        grid=(batch, n_pairs, steps),
        in_specs=[tok] * 6,
        out_specs=[tok, tok, mat, mat, vec],
        out_shape=[jax.ShapeDtypeStruct((n, MIX_WIDTH), BF16),
                   jax.ShapeDtypeStruct((n, MIX_WIDTH), F32),
                   jax.ShapeDtypeStruct((n_chunks * WKV_PAIR, MIX_WIDTH), BF16),
                   jax.ShapeDtypeStruct((n_chunks * WKV_PAIR, MIX_WIDTH), F32),
                   jax.ShapeDtypeStruct((n_chunks * 8, MIX_WIDTH), F32)],
        compiler_params=_params("parallel", "parallel", "parallel"),
        name="wkv_terms",
    )(r, k, v, an, bb, lw)


def _wkv_scan_kernel(qw_ref, y0_ref, e_ref, f_ref, pc_ref, bonus_ref, gate_ref, gng_ref, gnb_ref,
                     o_ref, state_ref):
    @pl.when(pl.program_id(2) == 0)
    def _():
        state_ref[...] = jnp.zeros_like(state_ref)

    n_pairs = state_ref.shape[1] // WKV_PAIR
    lane_lo = lax.broadcasted_iota(jnp.int32, (WKV_CHUNK, WKV_PAIR), 1) < RWKV_HEAD_DIM
    inv_n = 1.0 / RWKV_HEAD_DIM
    def head_mean(x):
        lo = jnp.sum(jnp.where(lane_lo, x, 0.0), axis=-1, keepdims=True)
        hi = jnp.sum(jnp.where(lane_lo, 0.0, x), axis=-1, keepdims=True)
        return jnp.where(lane_lo, lo, hi) * inv_n

    slices = [slice(p * WKV_PAIR, (p + 1) * WKV_PAIR) for p in range(n_pairs)]
    states = [state_ref[:, sl] for sl in slices]
    for j in range(qw_ref.shape[0] // WKV_CHUNK):
        rows = slice(j * WKV_CHUNK, (j + 1) * WKV_CHUNK)
        mrows = slice(j * WKV_PAIR, (j + 1) * WKV_PAIR)
        states_b = [s.astype(BF16) for s in states]
        ys = [_dot_nt(qw_ref[rows, sl], sb) for sl, sb in zip(slices, states_b)]
        se = [_dot(sb, e_ref[mrows, sl]) for sl, sb in zip(slices, states_b)]
        states = [s * pc_ref[8 * j:8 * j + 1, sl] + x + f_ref[mrows, sl] for sl, s, x in zip(slices, states, se)]
        for sl, y in zip(slices, ys):
            y = y + y0_ref[rows, sl]
            yc = y - head_mean(y)
            var = head_mean(yc * yc)
            yn = yc * lax.rsqrt(var + RWKV_GN_EPS) * gng_ref[:, sl] + gnb_ref[:, sl]
            o_ref[rows, sl] = ((yn + bonus_ref[rows, sl].astype(F32)) * gate_ref[rows, sl].astype(F32)
                               ).astype(o_ref.dtype)
    for sl, s in zip(slices, states):
        state_ref[:, sl] = s


def _wkv_scan(qw, y0, e, f, pc, bonus, gate, gn_g, gn_b, batch, seq):
    n = batch * seq
    cps = _pick_tile(seq // WKV_CHUNK, WKV_SCAN_CHUNKS_PER_STEP, 1)
    c = WKV_CHUNK * cps
    nc = seq // c
    width = MIX_WIDTH // 2
    groups = MIX_WIDTH // width
    tok = pl.BlockSpec((c, width), lambda b, g, i: (b * nc + i, g))
    mat = pl.BlockSpec((WKV_PAIR * cps, width), lambda b, g, i: (b * nc + i, g))
    vec = pl.BlockSpec((8 * cps, width), lambda b, g, i: (b * nc + i, g))
    par = pl.BlockSpec((1, width), lambda b, g, i: (0, g))
    return pl.pallas_call(
        _wkv_scan_kernel,
        grid=(batch, groups, nc),
        in_specs=[tok, tok, mat, mat, vec, tok, tok, par, par],
        out_specs=tok,
        out_shape=jax.ShapeDtypeStruct((n, MIX_WIDTH), BF16),
        scratch_shapes=[pltpu.VMEM((WKV_PAIR, width), F32)],
        compiler_params=_params("parallel", "parallel", "arbitrary"),
        name="wkv_scan",
    )(qw, y0, e, f, pc, bonus, gate, gn_g, gn_b)


def _route(x, w_hi, w_lo, b):
    x_hi, x_lo = _split2(x)
    logits = _dot(x_hi, w_hi) + (_dot(x_lo, w_hi) + _dot(x_hi, w_lo)) + b
    lane = lax.broadcasted_iota(jnp.int32, logits.shape, 1)
    neg_inf = -jnp.inf
    big = 4 * LANES

    def first_where(mask):
        return jnp.min(jnp.where(mask, lane, big), axis=-1, keepdims=True)

    gl = jnp.where(lane < N_GROUPS, logits, neg_inf)
    gmax = jnp.max(gl, axis=-1, keepdims=True)
    gidx = first_where(gl == gmax)
    p_group_sel = 1.0 / jnp.sum(jnp.exp(gl - gmax), axis=-1, keepdims=True)

    lo = N_GROUPS + EXPERTS_PER_GROUP * gidx
    emask = (lane >= lo) & (lane < lo + EXPERTS_PER_GROUP)
    el = jnp.where(emask, logits, neg_inf)
    emax = jnp.max(el, axis=-1, keepdims=True)
    pe = jnp.exp(el - emax)
    pe = pe / jnp.sum(pe, axis=-1, keepdims=True)
    pe = jnp.where(emask, pe, -1.0)
    p1 = jnp.max(pe, axis=-1, keepdims=True)
    i1 = first_where(pe == p1)
    pe2 = jnp.where(lane == i1, -1.0, pe)
    p2 = jnp.max(pe2, axis=-1, keepdims=True)
    i2 = first_where(pe2 == p2)
    denom = p1 + p2
    g1 = p_group_sel * p1 / denom
    g2 = p_group_sel * p2 / denom
    e1 = (i1 - N_GROUPS).astype(F32)
    e2 = (i2 - N_GROUPS).astype(F32)
    return jnp.where(lane == 0, e1, jnp.where(lane == 1, e2, jnp.where(lane == 2, g1, jnp.where(lane == 3, g2, 0.0))))


def _plan_kernel(e_ref, rank_ref, count_ref, carry_ref):
    @pl.when(pl.program_id(0) == 0)
    def _():
        carry_ref[...] = jnp.zeros_like(carry_ref)

    nb = e_ref.shape[-1]
    e = e_ref[0]
    onehot = jnp.where(lax.broadcasted_iota(jnp.int32, (N_EXPERTS, nb), 0) == e, 1.0, 0.0)
    r = lax.broadcasted_iota(jnp.int32, (nb, nb), 0)
    c = lax.broadcasted_iota(jnp.int32, (nb, nb), 1)
    before = jnp.where(r < c, 1.0, 0.0).astype(BF16)
    seen = _dot(onehot.astype(BF16), before) + carry_ref[:, 0:1]
    rank_ref[0] = jnp.sum(onehot * seen, axis=0, keepdims=True).astype(jnp.int32)
    carry_ref[...] = carry_ref[...] + jnp.sum(onehot, axis=1, keepdims=True)
    count_ref[...] = carry_ref[...]


def _dispatch_plan(e_flat):
    n_slots = e_flat.shape[0]
    nb = _pick_tile(n_slots, PLAN_BLOCK)
    steps = n_slots // nb
    rank, counts = pl.pallas_call(
        _plan_kernel,
        grid=(steps,),
        in_specs=[pl.BlockSpec((1, 1, nb), lambda i: (i, 0, 0))],
        out_specs=[pl.BlockSpec((1, 1, nb), lambda i: (i, 0, 0)),
                   pl.BlockSpec((N_EXPERTS, LANES), lambda i: (0, 0))],
        out_shape=[jax.ShapeDtypeStruct((steps, 1, nb), jnp.int32),
                   jax.ShapeDtypeStruct((N_EXPERTS, LANES), F32)],
        scratch_shapes=[pltpu.VMEM((N_EXPERTS, LANES), F32)],
        compiler_params=_params("arbitrary"),
        name="moe_plan",
    )(e_flat.reshape(steps, 1, nb))
    return rank.reshape(n_slots), counts[:, 0].astype(jnp.int32)


def _to_token_tiles(dst_ref, value):
    rows = value.shape[0]
    for j in range(TOKEN_TILE_ROWS):
        dst_ref[pl.ds(j, rows, stride=TOKEN_TILE_ROWS), :] = value[:, j * LANES:(j + 1) * LANES]


def _token_tile_columns(src_ref, rows, j):
    return src_ref[pl.ds(j, rows, stride=TOKEN_TILE_ROWS), :]


def _row_copy(table_hbm, dst_ref, src_row, dst_row, sem):
    src = pl.multiple_of(src_row * TOKEN_TILE_ROWS, TOKEN_TILE_ROWS)
    dst = pl.multiple_of(dst_row * TOKEN_TILE_ROWS, TOKEN_TILE_ROWS)
    return pltpu.make_async_copy(table_hbm.at[pl.ds(src, TOKEN_TILE_ROWS)],
                                 dst_ref.at[pl.ds(dst, TOKEN_TILE_ROWS)], sem)


def _token_gather(wait, idx_ref, offset, stride, n_tokens, table_hbm, dst_ref, sem):
    def one(i, u):
        if wait:
            _row_copy(table_hbm, dst_ref, 0, i, sem).wait()
        else:
            _row_copy(table_hbm, dst_ref, idx_ref[0, 0, offset + stride * i], i, sem).start(priority=u % 2)

    if isinstance(n_tokens, int):
        for i in range(n_tokens):
            one(i, i)
        return

    def group(g, carry):
        for u in range(GATHER_GROUP):
            one(g * GATHER_GROUP + u, u)
        return carry

    assert GATHER_GROUP & (GATHER_GROUP - 1) == 0
    groups = lax.shift_right_logical(n_tokens + (GATHER_GROUP - 1), GATHER_GROUP.bit_length() - 1)
    lax.fori_loop(0, groups, group, 0)


def _expert_kernel(be_ref, nu_ref, nv_ref, idx_ref, idx_next_ref, x_hbm, wg_ref, wu_ref, wd_ref, o_ref,
                   wg_s, wu_s, wd_s, xb_s, xbuf, sem):
    i = pl.program_id(0)
    n_used = nu_ref[0]
    slot = i & 1

    def gather(wait, refs, blk, slot):
        _token_gather(wait, refs, 0, 1, nv_ref[blk], x_hbm, xbuf.at[slot], sem.at[slot])

    @pl.when(i == 0)
    def _():
        xbuf[...] = jnp.zeros_like(xbuf)
        gather(False, idx_ref, 0, 0)

    @pl.when(i + 1 < n_used)
    def _():
        gather(False, idx_next_ref, i + 1, 1 - slot)

    prev = be_ref[jnp.maximum(i - 1, 0)]
    changed = (i == 0) | (be_ref[i] != prev)

    @pl.when(changed)
    def _():
        wg_s[...] = wg_ref[0, 0].astype(BF16)
        wu_s[...] = wu_ref[0, 0].astype(BF16)
        wd_s[...] = wd_ref[0, 0].astype(BF16)

    used = i < n_used

    @pl.when(used)
    def _():
        gather(True, idx_ref, i, slot)
        for j in range(TOKEN_TILE_ROWS):
            xb_s[:, j * LANES:(j + 1) * LANES] = _token_tile_columns(xbuf.at[slot], MOE_BLOCK, j).astype(BF16)
        parts = MOE_ROW_SPLIT
        rows = MOE_BLOCK // parts
        xs = [xb_s[p * rows:(p + 1) * rows, :] for p in range(parts)]
        g = [_dot(x, wg_s[...]) for x in xs]
        u = [_dot(x, wu_s[...]) for x in xs]
        h = [((a * jax.nn.sigmoid(a)) * b).astype(BF16) for a, b in zip(g, u)]
        y = [_dot(x, wd_s[...]) for x in h]
        for p in range(parts):
            _to_token_tiles(o_ref.at[pl.ds(p * rows * TOKEN_TILE_ROWS, rows * TOKEN_TILE_ROWS)], y[p])

    @pl.when(jnp.logical_not(used))
    def _():
        o_ref[...] = jnp.zeros_like(o_ref)


def _experts(x_tiles, src_tok, block_expert, n_used, n_valid, w_gate, w_up, w_down, layer):
    cap = src_tok.shape[0]
    n_blk = cap // MOE_BLOCK
    wspec = lambda shape: pl.BlockSpec((1, 1) + shape, lambda i, be, nu, nv: (layer, be[i], 0, 0))
    idx_spec = lambda f: pl.BlockSpec((1, 1, MOE_BLOCK), lambda i, be, nu, nv: (f(i), 0, 0),
                                      memory_space=pltpu.SMEM)
    grid_spec = pltpu.PrefetchScalarGridSpec(
        num_scalar_prefetch=3,
        grid=(n_blk,),
        in_specs=[idx_spec(lambda i: i), idx_spec(lambda i: jnp.minimum(i + 1, n_blk - 1)),
                  pl.BlockSpec(memory_space=pl.ANY),
                  wspec((D_MODEL, D_EXPERT)), wspec((D_MODEL, D_EXPERT)), wspec((D_EXPERT, D_MODEL))],
        out_specs=pl.BlockSpec((MOE_BLOCK * TOKEN_TILE_ROWS, LANES), lambda i, be, nu, nv: (i, 0)),
        scratch_shapes=[pltpu.VMEM((D_MODEL, D_EXPERT), BF16),
                        pltpu.VMEM((D_MODEL, D_EXPERT), BF16),
                        pltpu.VMEM((D_EXPERT, D_MODEL), BF16),
                        pltpu.VMEM((MOE_BLOCK, D_MODEL), BF16),
                        pltpu.VMEM((2, MOE_BLOCK * TOKEN_TILE_ROWS, LANES), F32),
                        pltpu.SemaphoreType.DMA((2,))],
    )
    idx = src_tok.reshape(n_blk, 1, MOE_BLOCK)
    return pl.pallas_call(
        _expert_kernel,
        grid_spec=grid_spec,
        out_shape=jax.ShapeDtypeStruct((cap * TOKEN_TILE_ROWS, LANES), F32),
        compiler_params=_params("arbitrary"),
        name="moe_experts",
    )(block_expert, n_used, n_valid, idx, idx, x_tiles, w_gate, w_up, w_down)


def _combine_ln_kernel(idx_ref, idx_next_ref, ys_hbm, slab_ref, x_ref, g_ref, b_ref, xo_ref, xb_ref,
                       buf, y_s, sem):
    i = pl.program_id(0)
    slot = i & 1
    tm = x_ref.shape[0]

    def gather(wait, refs, slot):
        for k in range(TOP_K):
            _token_gather(wait, refs, k, TOP_K, tm, ys_hbm, buf.at[slot, k], sem.at[slot, k])

    @pl.when(i == 0)
    def _():
        gather(False, idx_ref, 0)

    @pl.when(i + 1 < pl.num_programs(0))
    def _():
        gather(False, idx_next_ref, 1 - slot)

    slab = slab_ref[...]
    lane = lax.broadcasted_iota(jnp.int32, slab.shape, 1)
    g1 = jnp.sum(jnp.where(lane == 2, slab, 0.0), axis=-1, keepdims=True)
    g2 = jnp.sum(jnp.where(lane == 3, slab, 0.0), axis=-1, keepdims=True)
    gather(True, idx_ref, slot)
    for j in range(TOKEN_TILE_ROWS):
        cols = slice(j * LANES, (j + 1) * LANES)
        f = (_token_tile_columns(buf.at[slot, 0], tm, j) * g1
             + _token_tile_columns(buf.at[slot, 1], tm, j) * g2)
        y_s[:, cols] = DEEPNORM_ALPHA * x_ref[:, cols] + f
    out = _layer_norm_rows(y_s[...], g_ref[...], b_ref[...])
    xo_ref[...] = out
    xb_ref[...] = out.astype(BF16)


def _combine_ln(ys, dest, slab, x, g, b):
    n = x.shape[0]
    tm = _pick_tile(n, 256, 8)
    steps = n // tm
    row = lambda i: (i, 0)
    fixed = lambda i: (0, 0)
    idx_spec = lambda f: pl.BlockSpec((1, 1, TOP_K * tm), lambda i: (f(i), 0, 0), memory_space=pltpu.SMEM)
    idx = dest.reshape(steps, 1, TOP_K * tm)
    return pl.pallas_call(
        _combine_ln_kernel,
        grid=(steps,),
        in_specs=[idx_spec(lambda i: i), idx_spec(lambda i: jnp.minimum(i + 1, steps - 1)),
                  pl.BlockSpec(memory_space=pl.ANY),
                  pl.BlockSpec((tm, LANES), row),
                  pl.BlockSpec((tm, D_MODEL), row),
                  pl.BlockSpec((1, D_MODEL), fixed),
                  pl.BlockSpec((1, D_MODEL), fixed)],
        out_specs=[pl.BlockSpec((tm, D_MODEL), row), pl.BlockSpec((tm, D_MODEL), row)],
        out_shape=[jax.ShapeDtypeStruct((n, D_MODEL), F32), jax.ShapeDtypeStruct((n, D_MODEL), BF16)],
        scratch_shapes=[pltpu.VMEM((2, TOP_K, tm * TOKEN_TILE_ROWS, LANES), F32),
                        pltpu.VMEM((tm, D_MODEL), F32),
                        pltpu.SemaphoreType.DMA((2, TOP_K))],
        compiler_params=_params("arbitrary"),
        name="moe_combine_ln",
    )(idx, idx, ys, slab, x, g, b)


def _router_params(rg_w, rg_b, re_w, re_b):
    pad_cols = LANES - N_GROUPS - N_EXPERTS
    w_router = jnp.concatenate([rg_w, re_w, jnp.zeros((D_MODEL, pad_cols), F32)], axis=1)
    b_router = jnp.concatenate([rg_b, re_b, jnp.zeros((pad_cols,), F32)])[None, :]
    w_hi = w_router.astype(BF16)
    w_lo = (w_router - w_hi.astype(F32)).astype(BF16)
    return jnp.stack([w_hi, w_lo]), b_router


def _moe_ln(x, x_tiles, slab, w_gate, w_up, w_down, layer, ln_g, ln_b):
    n = x.shape[0]
    n_slots = n * TOP_K

    e_flat = slab[:, :TOP_K].astype(jnp.int32).reshape(n_slots)
    rank, counts = _dispatch_plan(e_flat)
    padded = (counts + MOE_BLOCK - 1) // MOE_BLOCK * MOE_BLOCK
    pad_end = jnp.cumsum(padded)
    pad_start = pad_end - padded
    dest = pad_start[e_flat] + rank
    cap = n_slots + N_EXPERTS * MOE_BLOCK
    n_blk = cap // MOE_BLOCK
    src_tok = jnp.zeros((cap,), jnp.int32).at[dest].set(jnp.arange(n_slots, dtype=jnp.int32) // TOP_K)
    block_start = jnp.arange(n_blk, dtype=jnp.int32) * MOE_BLOCK
    block_expert = jnp.minimum(jnp.sum((pad_end[None, :] <= block_start[:, None]).astype(jnp.int32), axis=1),
                               N_EXPERTS - 1)
    n_used = (pad_end[-1:] // MOE_BLOCK).astype(jnp.int32)
    n_valid = jnp.clip(counts[block_expert] - (block_start - pad_start[block_expert]), 0, MOE_BLOCK)
    n_valid = jnp.where(block_start < pad_end[-1], n_valid, 0).astype(jnp.int32)

    ys = _experts(x_tiles, src_tok, block_expert, n_used, n_valid, w_gate, w_up, w_down, layer)
    return _combine_ln(ys, dest, slab, x, ln_g[None, :], ln_b[None, :])


def _rwkv_mixer(x_bf, w_in, mu, w0, w2, a0, a2, g2, k_k, k_a, r_k, gn_g, gn_b, batch, seq):
    c3 = 3 * MIX_WIDTH
    lo_w, lo_a, lo_g = c3, c3 + DECAY_LORA, c3 + DECAY_LORA + AAA_LORA
    pad = LANES - DECAY_LORA

    def permute_cols(t, with_mem):
        zeros = jnp.zeros(t.shape[:-1] + (pad,), t.dtype)
        parts = [t[..., :c3]]
        if with_mem:
            parts.append(t[..., RWKV_COLS:])
        parts += [t[..., lo_w:lo_a], zeros, t[..., lo_a:lo_g], zeros, t[..., lo_g:RWKV_COLS]]
        return jnp.concatenate(parts, axis=-1)

    w_perm = permute_cols(w_in, True).astype(BF16)
    mu_perm = permute_cols(mu[None, :], False)
    mu_main, mu_lora = mu_perm[:, :c3], mu_perm[:, c3:]
    z = _matmul(x_bf, w_perm, BF16, "rwkv_in_proj")

    zrow = jnp.zeros((pad, MIX_WIDTH), F32)
    w2p = jnp.concatenate([w2, zrow], axis=0).astype(BF16)
    a2p = jnp.concatenate([a2, zrow], axis=0).astype(BF16)
    r, k, v, an, bb, lw, gate, bonus = _rwkv_prep(
        z, mu_main, mu_lora, w2p, a2p, g2.astype(BF16), w0[None, :], a0[None, :], k_k[None, :], k_a[None, :],
        r_k.reshape(1, MIX_WIDTH), batch, seq)
    qw, y0, e, f, pc = _wkv_terms(r, k, v, an, bb, lw, batch, seq)
    mix = _wkv_scan(qw, y0, e, f, pc, bonus, gate, gn_g[None, :], gn_b[None, :], batch, seq)
    return z, mix, c3 // COL_BLOCK


def kernel(x, mem, w_out, mem_wk, mem_wv, ln_mix_g, ln_mix_b, ln_ffn_g, ln_ffn_b, router_group_w, router_group_b, router_expert_w, router_expert_b, expert_w_gate, expert_w_up, expert_w_down, rwkv_w_in, rwkv_mu, rwkv_w0, rwkv_w2, rwkv_a0, rwkv_a2, rwkv_g2, rwkv_k_k, rwkv_k_a, rwkv_r_k, rwkv_gn_g, rwkv_gn_b, conv_w_in, conv_w, sb_w_in):
    batch, seq, d = x.shape
    assert d == D_MODEL and mem.shape == (batch, MEM_LEN, D_MODEL)
    n = batch * seq
    xf = x.reshape(n, D_MODEL)
    xb = xf.astype(BF16)
    mem_bf = mem.reshape(batch * MEM_LEN, D_MODEL).astype(BF16)
    q_block = 3 * MIX_WIDTH // COL_BLOCK

    for i in range(DEPTH):
        kind, j = i % N_MIXERS, i // N_MIXERS
        if kind == 0:
            z, mix, _ = _rwkv_mixer(xb, rwkv_w_in[j], rwkv_mu[j], rwkv_w0[j], rwkv_w2[j], rwkv_a0[j], rwkv_a2[j],
                                    rwkv_g2[j], rwkv_k_k[j], rwkv_k_a[j], rwkv_r_k[j], rwkv_gn_g[j], rwkv_gn_b[j],
                                    batch, seq)
        elif kind == 1:
            z = _matmul(xb, conv_w_in[j].astype(BF16), BF16, "conv_in_proj")
            mix = _short_conv(z, conv_w[j], batch, seq)
        else:
            z = _matmul(xb, sb_w_in[j].astype(BF16), BF16, "sb_in_proj")
            mix = _stick_breaking(z, batch, seq)
        w_kv = jnp.concatenate([mem_wk[i], mem_wv[i]], axis=1).astype(BF16)
        kv = _matmul(mem_bf, w_kv, BF16, "mem_kv_proj")
        mem_out = _mem_attention(z, kv, batch, seq, q_block)
        w_o = w_out[i].astype(BF16)
        w_router, b_router = _router_params(router_group_w[i], router_group_b[i],
                                            router_expert_w[i], router_expert_b[i])
        xf, x_tiles, slab = _outproj_ln(mix, mem_out, w_o[:MIX_WIDTH], w_o[MIX_WIDTH:], xf,
                                        ln_mix_g[i][None, :], ln_mix_b[i][None, :], w_router, b_router)
        xf, xb = _moe_ln(xf, x_tiles, slab, expert_w_gate, expert_w_up, expert_w_down, i,
                         ln_ffn_g[i], ln_ffn_b[i])
    return xf.reshape(batch, seq, D_MODEL)
```
